```python
import math
import jax, jax.numpy as jnp
from jax import lax
import numpy as np

D_MODEL = 1024
BATCH = 4
SEQ = 4096
DEPTH = 4

ROPE_THETA = 10000.0
LN_EPS = 1e-5
ATT_GROUPS = ((128, 1), (512, 4), (2048, 16))
ATT_HEADS_PER_GROUP = 4
ATT_HEADS = ATT_HEADS_PER_GROUP * len(ATT_GROUPS)
ATT_HEAD_DIM = 128
ATT_BLOCK = 128
ATT_OUT = ATT_HEADS_PER_GROUP * ATT_HEAD_DIM
LRU_WIDTH = D_MODEL
LRU_BLOCKS = 16
LRU_BLOCK_DIM = LRU_WIDTH // LRU_BLOCKS
CONV_WIDTH = 4
LRU_C = 8.0
RET_HEADS = 4
RET_DK = 256
RET_DV = 256
RET_CHUNK = 128
PEER_HEADS = 8
PEER_NKEYS = 128
PEER_EXPERTS = PEER_NKEYS * PEER_NKEYS
PEER_QDIM = 256
PEER_TOPK = 16
PEER_BLOCK = 128
DEEPNORM_ALPHA = (2 * DEPTH) ** 0.25
DEEPNORM_BETA = (8 * DEPTH) ** -0.25
IN_SPLITS = (ATT_HEADS * ATT_HEAD_DIM,) * 3 + (LRU_WIDTH,) * 2 + (RET_HEADS * RET_DK, RET_HEADS * RET_DK, RET_HEADS * RET_DV, RET_HEADS * RET_DV) + (D_MODEL,) * 3
D_IN = sum(IN_SPLITS)
BR_SPLITS = (ATT_OUT, LRU_WIDTH, RET_HEADS * RET_DV)
D_BR = sum(BR_SPLITS)

kernel_name = "hybrid_dilated_lru_retention_peer"


def layer_norm(x, g, b):
    xf = x.astype(jnp.float32)
    mu = jnp.mean(xf, axis=-1, keepdims=True)
    var = jnp.mean(jnp.square(xf - mu), axis=-1, keepdims=True)
    return ((xf - mu) * lax.rsqrt(var + LN_EPS) * g + b).astype(x.dtype)


def rope_tables(seq, dim):
    inv = ROPE_THETA ** (-jnp.arange(0, dim, 2, dtype=jnp.float32) / dim)
    ang = jnp.arange(seq, dtype=jnp.float32)[:, None] * inv[None, :]
    return jnp.cos(ang), jnp.sin(ang)


def apply_rope(t, cos, sin):
    half = t.shape[-1] // 2
    t1 = t[..., :half].astype(jnp.float32)
    t2 = t[..., half:].astype(jnp.float32)
    c = cos[None, :, None, :]
    s = sin[None, :, None, :]
    return jnp.concatenate([t1 * c - t2 * s, t1 * s + t2 * c], axis=-1).astype(t.dtype)


def dilated_group(q, k, v, window, dil):
    B, S, H, E = q.shape
    span = dil * ATT_BLOCK
    Sp = -(-S // span) * span
    L = Sp // dil
    nb = L // ATT_BLOCK
    sub_win = window // dil

    def prep(t):
        t = jnp.pad(t, ((0, 0), (0, Sp - S), (0, 0), (0, 0)))
        t = t.reshape(B, L, dil, H, E).transpose(0, 2, 3, 1, 4)
        return t.reshape(B, dil, H, nb, ATT_BLOCK, E)

    qb, kb, vb = prep(q), prep(k), prep(v)
    pad_prev = ((0, 0), (0, 0), (0, 0), (1, 0), (0, 0), (0, 0))
    kcat = jnp.concatenate([jnp.pad(kb[:, :, :, :-1], pad_prev), kb], axis=4)
    vcat = jnp.concatenate([jnp.pad(vb[:, :, :, :-1], pad_prev), vb], axis=4)
    s = jnp.einsum('bdhnqe,bdhnke->bdhnqk', qb, kcat).astype(jnp.float32) * (E ** -0.5)
    qi = jnp.arange(ATT_BLOCK)[:, None] + ATT_BLOCK
    kj = jnp.arange(2 * ATT_BLOCK)[None, :]
    delta = qi - kj
    blk = jnp.arange(nb)[:, None, None]
    mask = (delta >= 0)[None] & (delta <= sub_win)[None] & (blk * ATT_BLOCK + kj[None] - ATT_BLOCK >= 0)
    s = jnp.where(mask, s, -jnp.inf)
    lse = jax.nn.logsumexp(s, axis=-1, keepdims=True)
    p = jnp.exp(s - lse)
    o = jnp.einsum('bdhnqk,bdhnke->bdhnqe', p.astype(vcat.dtype), vcat)
    o = o.reshape(B, dil, H, L, E).transpose(0, 3, 1, 2, 4).reshape(B, Sp, H, E)[:, :S]
    lse = lse[..., 0].reshape(B, dil, H, L).transpose(0, 3, 1, 2).reshape(B, Sp, H)[:, :S]
    return o, lse


def rg_lru(xc, wa, ba, wx, bx, lam):
    B, S, C = xc.shape
    xr = xc.reshape(B, S, LRU_BLOCKS, LRU_BLOCK_DIM)
    r = jax.nn.sigmoid((jnp.einsum('bsgi,gio->bsgo', xr, wa).reshape(B, S, C) + ba).astype(jnp.float32))
    i = jax.nn.sigmoid((jnp.einsum('bsgi,gio->bsgo', xr, wx).reshape(B, S, C) + bx).astype(jnp.float32))
    log_a = -LRU_C * r * jax.nn.softplus(-lam.astype(jnp.float32))
    a = jnp.exp(log_a)
    b = jnp.sqrt(-jnp.expm1(2.0 * log_a)) * (i * xc.astype(jnp.float32))

    def combine(left, right):
        a1, b1 = left
        a2, b2 = right
        return a1 * a2, a2 * b1 + b2

    _, h = lax.associative_scan(combine, (a, b), axis=1)
    return h


def retention(q, k, v):
    B, S, H, Dk = q.shape
    Dv = v.shape[-1]
    n = S // RET_CHUNK
    log_g = jnp.log(1.0 - 2.0 ** (-5.0 - jnp.arange(H, dtype=jnp.float32)))
    idx = jnp.arange(RET_CHUNK, dtype=jnp.float32)
    diff = idx[:, None] - idx[None, :]
    dmat = jnp.where(diff >= 0, jnp.exp(jnp.maximum(diff, 0.0)[None] * log_g[:, None, None]), 0.0)
    xi = jnp.exp((idx[None] + 1.0) * log_g[:, None])[..., None]
    zeta = jnp.exp((RET_CHUNK - 1.0 - idx[None]) * log_g[:, None])[..., None]
    g_chunk = jnp.exp(RET_CHUNK * log_g)[:, None, None]

    def chunks(t):
        return t.reshape(B, n, RET_CHUNK, H, t.shape[-1]).transpose(1, 0, 3, 2, 4)

    def step(R, inp):
        qc, kc, vc = inp
        inner = jnp.einsum('bhij,bhjv->bhiv', jnp.einsum('bhik,bhjk->bhij', qc, kc) * dmat, vc)
        cross = jnp.einsum('bhik,bhkv->bhiv', qc, R) * xi
        R_new = g_chunk * R + jnp.einsum('bhjk,bhjv->bhkv', kc * zeta, vc)
        return R_new, inner + cross

    R0 = jnp.zeros((B, H, Dk, Dv), jnp.float32)
    _, out = lax.scan(step, R0, (chunks(q), chunks(k), chunks(v)))
    return out.transpose(1, 0, 3, 2, 4).reshape(B, S, H, Dv)


def token_mixers(x, w_in, conv_w, conv_b, lru_wa, lru_ba, lru_wx, lru_bx, lru_lambda, w_branch, w_out, cos_a, sin_a, cos_r, sin_r):
    B, S, _ = x.shape
    proj = jnp.einsum('bsd,de->bse', x, w_in)
    aq, ak, av, lx, lg, rq, rk, rv, rg, ga, gb, gc = jnp.split(proj, np.cumsum(IN_SPLITS)[:-1].tolist(), axis=-1)

    aq = apply_rope(aq.reshape(B, S, ATT_HEADS, ATT_HEAD_DIM), cos_a, sin_a)
    ak = apply_rope(ak.reshape(B, S, ATT_HEADS, ATT_HEAD_DIM), cos_a, sin_a)
    av = av.reshape(B, S, ATT_HEADS, ATT_HEAD_DIM)
    outs, lses = [], []
    for g, (win, dil) in enumerate(ATT_GROUPS):
        sl = slice(g * ATT_HEADS_PER_GROUP, (g + 1) * ATT_HEADS_PER_GROUP)
        o, l = dilated_group(aq[:, :, sl], ak[:, :, sl], av[:, :, sl], win, dil)
        outs.append(o)
        lses.append(l)
    wgt = jax.nn.softmax(jnp.stack(lses), axis=0)
    y_a = jnp.sum(wgt[..., None] * jnp.stack(outs).astype(jnp.float32), axis=0)
    y_a = y_a.reshape(B, S, ATT_OUT).astype(x.dtype)

    xc = lax.conv_general_dilated(lx, conv_w[:, None, :], window_strides=(1,), padding=[(CONV_WIDTH - 1, 0)],
                                  dimension_numbers=('NWC', 'WIO', 'NWC'), feature_group_count=LRU_WIDTH) + conv_b
    h = rg_lru(xc, lru_wa, lru_ba, lru_wx, lru_bx, lru_lambda)
    y_b = (h * jax.nn.gelu(lg.astype(jnp.float32), approximate=False)).astype(x.dtype)

    rq = apply_rope(rq.reshape(B, S, RET_HEADS, RET_DK), cos_r, sin_r).astype(jnp.float32)
    rk = apply_rope(rk.reshape(B, S, RET_HEADS, RET_DK), cos_r, sin_r).astype(jnp.float32) * (RET_DK ** -0.5)
    rv = rv.reshape(B, S, RET_HEADS, RET_DV).astype(jnp.float32)
    ret = retention(rq, rk, rv)
    mu = jnp.mean(ret, axis=-1, keepdims=True)
    var = jnp.mean(jnp.square(ret - mu), axis=-1, keepdims=True)
    ret = ((ret - mu) * lax.rsqrt(var + LN_EPS)).reshape(B, S, RET_HEADS * RET_DV)
    y_c = (jax.nn.silu(rg.astype(jnp.float32)) * ret).astype(x.dtype)

    wb_a, wb_b, wb_c = jnp.split(w_branch, np.cumsum(BR_SPLITS)[:-1].tolist(), axis=0)
    merged = (jax.nn.sigmoid(ga) * (y_a @ wb_a) + jax.nn.sigmoid(gb) * (y_b @ wb_b)
              + jax.nn.sigmoid(gc) * (y_c @ wb_c))
    return merged @ w_out


def peer_ffn(x, wq, k1, k2, U, V):
    B, S, D = x.shape
    xt = x.reshape(-1, D)
    N = xt.shape[0]
    q = (xt @ wq).astype(jnp.float32).reshape(N, PEER_HEADS, PEER_QDIM)
    half = PEER_QDIM // 2
    s1 = jnp.einsum('nhc,hkc->nhk', q[..., :half], k1.astype(jnp.float32))
    s2 = jnp.einsum('nhc,hkc->nhk', q[..., half:], k2.astype(jnp.float32))
    v1, i1 = lax.top_k(s1, PEER_TOPK)
    v2, i2 = lax.top_k(s2, PEER_TOPK)
    cand = (v1[..., :, None] + v2[..., None, :]).reshape(N, PEER_HEADS, PEER_TOPK * PEER_TOPK)
    cv, ci = lax.top_k(cand, PEER_TOPK)
    e = (jnp.take_along_axis(i1, ci // PEER_TOPK, axis=-1) * PEER_NKEYS
         + jnp.take_along_axis(i2, ci % PEER_TOPK, axis=-1))
    g = jax.nn.softmax(cv, axis=-1)
    nb = N // PEER_BLOCK

    def block(args):
        xb, eb, gbk = args
        u = U[eb]
        pre = jnp.einsum('thkd,td->thk', u, xb).astype(jnp.float32)
        act = (jax.nn.gelu(pre, approximate=False) * gbk).astype(V.dtype)
        return jnp.einsum('thk,thkd->td', act, V[eb])

    y = lax.map(block, (xt.reshape(nb, PEER_BLOCK, D), e.reshape(nb, PEER_BLOCK, PEER_HEADS, PEER_TOPK),
                        g.reshape(nb, PEER_BLOCK, PEER_HEADS, PEER_TOPK)))
    return y.reshape(B, S, D).astype(x.dtype)


def setup_inputs(seed: int = 0) -> dict:
    key = jax.random.key(seed)
    ks = jax.random.split(key, 24)
    D = D_MODEL
    nrm = jax.random.normal
    x = nrm(ks[0], (BATCH, SEQ, D), jnp.float32)
    w_in = nrm(ks[1], (DEPTH, D, D_IN), jnp.float32) * D ** -0.5
    conv_w = nrm(ks[2], (DEPTH, CONV_WIDTH, LRU_WIDTH), jnp.float32) * CONV_WIDTH ** -0.5
    conv_b = 0.01 * nrm(ks[3], (DEPTH, LRU_WIDTH), jnp.float32)
    lru_wa = nrm(ks[4], (DEPTH, LRU_BLOCKS, LRU_BLOCK_DIM, LRU_BLOCK_DIM), jnp.float32) * LRU_BLOCK_DIM ** -0.5
    lru_ba = 0.01 * nrm(ks[5], (DEPTH, LRU_WIDTH), jnp.float32)
    lru_wx = nrm(ks[6], (DEPTH, LRU_BLOCKS, LRU_BLOCK_DIM, LRU_BLOCK_DIM), jnp.float32) * LRU_BLOCK_DIM ** -0.5
    lru_bx = 0.01 * nrm(ks[7], (DEPTH, LRU_WIDTH), jnp.float32)
    a_c = jax.random.uniform(ks[8], (DEPTH, LRU_WIDTH), jnp.float32, minval=0.9, maxval=0.999)
    a0 = a_c ** (1.0 / LRU_C)
    lru_lambda = jnp.log(a0) - jnp.log1p(-a0)
    w_branch = jnp.concatenate([
        nrm(ks[9], (DEPTH, BR_SPLITS[0], D), jnp.float32) * BR_SPLITS[0] ** -0.5,
        nrm(ks[10], (DEPTH, BR_SPLITS[1], D), jnp.float32) * BR_SPLITS[1] ** -0.5,
        nrm(ks[11], (DEPTH, BR_SPLITS[2], D), jnp.float32) * BR_SPLITS[2] ** -0.5], axis=1)
    w_out = nrm(ks[12], (DEPTH, D, D), jnp.float32) * D ** -0.5 * DEEPNORM_BETA
    ln1_g = 1.0 + 0.02 * nrm(ks[13], (DEPTH, D), jnp.float32)
    ln1_b = 0.02 * nrm(ks[14], (DEPTH, D), jnp.float32)
    peer_wq = nrm(ks[15], (DEPTH, D, PEER_HEADS * PEER_QDIM), jnp.float32) * D ** -0.5
    peer_k1 = nrm(ks[16], (DEPTH, PEER_HEADS, PEER_NKEYS, PEER_QDIM // 2), jnp.float32) * (PEER_QDIM // 2) ** -0.5
    peer_k2 = nrm(ks[17], (DEPTH, PEER_HEADS, PEER_NKEYS, PEER_QDIM // 2), jnp.float32) * (PEER_QDIM // 2) ** -0.5
    peer_u = nrm(ks[18], (DEPTH, PEER_EXPERTS, D), jnp.float32) * D ** -0.5
    peer_v = nrm(ks[19], (DEPTH, PEER_EXPERTS, D), jnp.float32) * PEER_HEADS ** -0.5 * DEEPNORM_BETA
    ln2_g = 1.0 + 0.02 * nrm(ks[20], (DEPTH, D), jnp.float32)
    ln2_b = 0.02 * nrm(ks[21], (DEPTH, D), jnp.float32)
    return {"x": x, "w_in": w_in, "conv_w": conv_w, "conv_b": conv_b, "lru_wa": lru_wa, "lru_ba": lru_ba,
            "lru_wx": lru_wx, "lru_bx": lru_bx, "lru_lambda": lru_lambda, "w_branch": w_branch, "w_out": w_out,
            "ln1_g": ln1_g, "ln1_b": ln1_b, "peer_wq": peer_wq, "peer_k1": peer_k1, "peer_k2": peer_k2,
            "peer_u": peer_u, "peer_v": peer_v, "ln2_g": ln2_g, "ln2_b": ln2_b}


def reference(x, w_in, conv_w, conv_b, lru_wa, lru_ba, lru_wx, lru_bx, lru_lambda, w_branch, w_out,
              ln1_g, ln1_b, peer_wq, peer_k1, peer_k2, peer_u, peer_v, ln2_g, ln2_b):
    S = x.shape[1]
    cos_a, sin_a = rope_tables(S, ATT_HEAD_DIM)
    cos_r, sin_r = rope_tables(S, RET_DK)
    for l in range(DEPTH):
        m = token_mixers(x, w_in[l], conv_w[l], conv_b[l], lru_wa[l], lru_ba[l], lru_wx[l], lru_bx[l],
                         lru_lambda[l], w_branch[l], w_out[l], cos_a, sin_a, cos_r, sin_r)
        x = layer_norm(DEEPNORM_ALPHA * x + m, ln1_g[l], ln1_b[l])
        f = peer_ffn(x, peer_wq[l], peer_k1[l], peer_k2[l], peer_u[l], peer_v[l])
        x = layer_norm(DEEPNORM_ALPHA * x + f, ln2_g[l], ln2_b[l])
    return x
```

```python
import functools
import math

import numpy as np
import jax
import jax.numpy as jnp
from jax import lax
from jax.experimental import pallas as pl
from jax.experimental.pallas import tpu as pltpu

F32 = jnp.float32
BF16 = jnp.bfloat16

D_MODEL = 1024
DEPTH = 4
ROPE_THETA = 10000.0
LN_EPS = 1e-5
ATT_GROUPS = ((128, 1), (512, 4), (2048, 16))
ATT_HPG = 4
ATT_E = 128
ATT_BLK = 128
ATT_W = ATT_HPG * ATT_E
LRU_BLOCKS = 16
CONV_WIDTH = 4
LRU_C = 8.0
RET_HEADS = 4
RET_DK = 256
RET_CHUNK = 128
PEER_HEADS = 8
PEER_NKEYS = 128
PEER_QDIM = 256
PEER_TOPK = 16
ALPHA = (2 * DEPTH) ** 0.25

D_IN = 13824
COL_LX, COL_LG, COL_RQ, COL_RK, COL_RV, COL_RG, COL_GA, COL_GB, COL_GC = (2 * i for i in range(9))
COL_AQ, COL_AK, COL_AV = 18, 21, 24
NCOL512 = D_IN // 512

VMEM_LIMIT = 56 * 1024 * 1024
NEG_INF = float("-inf")


def _cparams(sem):
    return pltpu.CompilerParams(dimension_semantics=sem, vmem_limit_bytes=VMEM_LIMIT)


def _layer_norm(z, g, b):
    mu = jnp.mean(z, axis=-1, keepdims=True)
    zc = z - mu
    var = jnp.mean(zc * zc, axis=-1, keepdims=True)
    return zc * lax.rsqrt(var + LN_EPS) * g + b


def _gelu(x):
    return 0.5 * x * (1.0 + lax.erf(x * np.float32(math.sqrt(0.5))))


def _mm_kernel(x_ref, w_ref, o_ref):
    o_ref[...] = jnp.dot(x_ref[...].astype(BF16), w_ref[...],
                         preferred_element_type=F32).astype(o_ref.dtype)


def _matmul(x, w, out_dtype, tm=512, tn=1536):
    n, k = x.shape
    m = w.shape[1]
    return pl.pallas_call(
        _mm_kernel,
        grid=(m // tn, n // tm),
        in_specs=[pl.BlockSpec((tm, k), lambda j, i: (i, 0)),
                  pl.BlockSpec((k, tn), lambda j, i: (0, j))],
        out_specs=pl.BlockSpec((tm, tn), lambda j, i: (i, j)),
        out_shape=jax.ShapeDtypeStruct((n, m), out_dtype),
        compiler_params=_cparams(("parallel", "parallel")),
        name="in_proj",
    )(x, w)


def _att_kernel(q_ref, kp_ref, kc_ref, vp_ref, vc_ref, cc_ref, sc_ref, cp_ref, sp_ref,
                o_ref, lse_ref):
    n = pl.program_id(2)
    cos_c, sin_c = cc_ref[...], sc_ref[...]
    cos_p, sin_p = cp_ref[...], sp_ref[...]

    def rope(t, c, s):
        return t * c + pltpu.roll(t, ATT_E // 2, axis=1) * s

    qi = lax.broadcasted_iota(jnp.int32, (ATT_BLK, 2 * ATT_BLK), 0)
    kj = lax.broadcasted_iota(jnp.int32, (ATT_BLK, 2 * ATT_BLK), 1)
    delta = qi + ATT_BLK - kj
    inside = jnp.where(delta >= 0, jnp.where(delta <= ATT_BLK, 1, 0), 0)
    has_prev = jnp.where(n > 0, 1, 0)
    valid = jnp.where(kj >= ATT_BLK, inside, inside * has_prev) > 0

    for h in range(ATT_HPG):
        sl = slice(h * ATT_E, (h + 1) * ATT_E)
        q = rope(q_ref[0, :, sl].astype(F32), cos_c, sin_c).astype(BF16)
        kp = rope(kp_ref[0, :, sl].astype(F32), cos_p, sin_p).astype(BF16)
        kc = rope(kc_ref[0, :, sl].astype(F32), cos_c, sin_c).astype(BF16)
        kcat = jnp.concatenate([kp, kc], axis=0)
        vcat = jnp.concatenate([vp_ref[0, :, sl], vc_ref[0, :, sl]], axis=0).astype(BF16)
        s = lax.dot_general(q, kcat, (((1,), (1,)), ((), ())), preferred_element_type=F32)
        s = jnp.where(valid, s * np.float32(ATT_E ** -0.5), NEG_INF)
        m = jnp.max(s, axis=-1, keepdims=True)
        p = jnp.exp(s - m)
        l = jnp.sum(p, axis=-1, keepdims=True)
        o = jnp.dot(p.astype(BF16), vcat, preferred_element_type=F32)
        o_ref[0, :, sl] = (o / l).astype(o_ref.dtype)
        lse_ref[0, :, sl] = jnp.broadcast_to(m + jnp.log(l), (ATT_BLK, ATT_E))


def _attention_group(proj, cos_t, sin_t, g, dil, batch, seq):
    L = seq // dil
    nb = L // ATT_BLK
    pv = proj.reshape(batch, L, dil * D_IN)
    cv = cos_t.reshape(L, dil * ATT_E)
    sv = sin_t.reshape(L, dil * ATT_E)

    def cur(col):
        return pl.BlockSpec((1, ATT_BLK, ATT_W), lambda b, r, n: (b, n, r * NCOL512 + col + g))

    def prev(col):
        return pl.BlockSpec((1, ATT_BLK, ATT_W),
                            lambda b, r, n: (b, jnp.maximum(n - 1, 0), r * NCOL512 + col + g))

    tab_c = pl.BlockSpec((ATT_BLK, ATT_E), lambda b, r, n: (n, r))
    tab_p = pl.BlockSpec((ATT_BLK, ATT_E), lambda b, r, n: (jnp.maximum(n - 1, 0), r))
    out_spec = pl.BlockSpec((1, ATT_BLK, ATT_W), lambda b, r, n: (b, n, r))
    o, lse = pl.pallas_call(
        _att_kernel,
        grid=(batch, dil, nb),
        in_specs=[cur(COL_AQ), prev(COL_AK), cur(COL_AK), prev(COL_AV), cur(COL_AV),
                  tab_c, tab_c, tab_p, tab_p],
        out_specs=[out_spec, out_spec],
        out_shape=[jax.ShapeDtypeStruct((batch, L, dil * ATT_W), F32)] * 2,
        compiler_params=_cparams(("parallel", "parallel", "parallel")),
        name=f"dilated_attention_d{dil}",
    )(pv, pv, pv, pv, pv, cv, sv, cv, sv)
    return o.reshape(batch * seq, ATT_W), lse.reshape(batch * seq, ATT_W)


LRU_TS = 256
LRU_HALO = 8


def _lru_kernel(lx_ref, lg_ref, cw_ref, cb_ref, wg_ref, bg_ref, lam_ref, y_ref, ext_ref, h_ref):
    s_idx = pl.program_id(1)
    ts, c = lx_ref.shape[1], lx_ref.shape[2]

    @pl.when(s_idx == 0)
    def _():
        ext_ref[0:LRU_HALO, :] = jnp.zeros((LRU_HALO, c), F32)
        h_ref[...] = jnp.zeros_like(h_ref)

    ext_ref[LRU_HALO:, :] = lx_ref[0].astype(F32)
    xc = jnp.broadcast_to(cb_ref[...], (ts, c))
    for w in range(CONV_WIDTH):
        off = LRU_HALO - (CONV_WIDTH - 1) + w
        xc = xc + ext_ref[off:off + ts, :] * cw_ref[w:w + 1, :]
    ext_ref[0:LRU_HALO, :] = ext_ref[ts:ts + LRU_HALO, :]

    gates = jnp.dot(xc.astype(BF16), wg_ref[...], preferred_element_type=F32) + bg_ref[...]
    r = jax.nn.sigmoid(gates[:, :c])
    i = jax.nn.sigmoid(gates[:, c:])
    z = -lam_ref[...]
    softplus = jnp.maximum(z, 0.0) + jnp.log1p(jnp.exp(-jnp.abs(z)))
    log_a = (-LRU_C) * r * softplus
    a = jnp.exp(log_a)
    b = jnp.sqrt(1.0 - a * a) * (i * xc)

    row = lax.broadcasted_iota(jnp.int32, (ts, c), 0)
    k = 1
    while k < ts:
        keep = row >= k
        a_s = jnp.where(keep, pltpu.roll(a, k, axis=0), 1.0)
        b_s = jnp.where(keep, pltpu.roll(b, k, axis=0), 0.0)
        b = a * b_s + b
        a = a * a_s
        k *= 2
    h = b + a * h_ref[...]
    h_ref[...] = h[ts - 1:ts, :]
    y_ref[0] = (h * _gelu(lg_ref[0].astype(F32))).astype(y_ref.dtype)


def _lru(proj3, conv_w, conv_b, wg, bg, lam, out_dtype):
    batch, seq, _ = proj3.shape
    c = D_MODEL
    row = lambda b, s: (0, 0)
    return pl.pallas_call(
        _lru_kernel,
        grid=(batch, seq // LRU_TS),
        in_specs=[pl.BlockSpec((1, LRU_TS, c), lambda b, s: (b, s, COL_LX // 2)),
                  pl.BlockSpec((1, LRU_TS, c), lambda b, s: (b, s, COL_LG // 2)),
                  pl.BlockSpec((CONV_WIDTH, c), row),
                  pl.BlockSpec((1, c), row),
                  pl.BlockSpec((c, 2 * c), row),
                  pl.BlockSpec((1, 2 * c), row),
                  pl.BlockSpec((1, c), row)],
        out_specs=pl.BlockSpec((1, LRU_TS, c), lambda b, s: (b, s, 0)),
        out_shape=jax.ShapeDtypeStruct((batch, seq, c), out_dtype),
        scratch_shapes=[pltpu.VMEM((LRU_TS + LRU_HALO, c), F32), pltpu.VMEM((1, c), F32)],
        compiler_params=_cparams(("parallel", "arbitrary")),
        name="rg_lru",
    )(proj3, proj3, conv_w, conv_b, wg, bg, lam)


def _ret_kernel(q_ref, k_ref, v_ref, g_ref, cos_ref, sin_ref, dmat_ref, xi_ref, zeta_ref, gch_ref,
                y_ref, r_ref):
    @pl.when(pl.program_id(1) == 0)
    def _():
        r_ref[...] = jnp.zeros_like(r_ref)

    cos, sin = cos_ref[...], sin_ref[...]
    half = RET_DK // 2

    def rope(t):
        t1, t2 = t[:, :half], t[:, half:]
        return jnp.concatenate([t1 * cos - t2 * sin, t1 * sin + t2 * cos], axis=1)

    for h in range(RET_HEADS):
        sl = slice(h * RET_DK, (h + 1) * RET_DK)
        q = rope(q_ref[0, :, sl].astype(F32)).astype(BF16)
        k = rope(k_ref[0, :, sl].astype(F32)) * np.float32(RET_DK ** -0.5)
        v = v_ref[0, :, sl].astype(BF16)
        state = r_ref[h]
        qk = lax.dot_general(q, k.astype(BF16), (((1,), (1,)), ((), ())), preferred_element_type=F32)
        inner = jnp.dot((qk * dmat_ref[h]).astype(BF16), v, preferred_element_type=F32)
        cross = jnp.dot(q, state.astype(BF16), preferred_element_type=F32) * xi_ref[h]
        kz = (k * zeta_ref[h]).astype(BF16)
        r_ref[h] = gch_ref[h] * state + lax.dot_general(
            kz, v, (((0,), (0,)), ((), ())), preferred_element_type=F32)
        ret = inner + cross
        mu = jnp.mean(ret, axis=-1, keepdims=True)
        rc = ret - mu
        var = jnp.mean(rc * rc, axis=-1, keepdims=True)
        gate = g_ref[0, :, sl].astype(F32)
        y_ref[0, :, sl] = (gate * jax.nn.sigmoid(gate) * (rc * lax.rsqrt(var + LN_EPS))).astype(y_ref.dtype)


def _retention(proj3, cos_r, sin_r, dmat, xi, zeta, gch, out_dtype):
    batch, seq, _ = proj3.shape
    c = RET_HEADS * RET_DK
    blk = lambda col: pl.BlockSpec((1, RET_CHUNK, c), lambda b, n: (b, n, col // 2))
    tab = pl.BlockSpec((RET_CHUNK, RET_DK // 2), lambda b, n: (n, 0))
    const3 = lambda shape: pl.BlockSpec(shape, lambda b, n: (0, 0, 0))
    return pl.pallas_call(
        _ret_kernel,
        grid=(batch, seq // RET_CHUNK),
        in_specs=[blk(COL_RQ), blk(COL_RK), blk(COL_RV), blk(COL_RG), tab, tab,
                  const3((RET_HEADS, RET_CHUNK, RET_CHUNK)),
                  const3((RET_HEADS, RET_CHUNK, 1)),
                  const3((RET_HEADS, RET_CHUNK, 1)),
                  const3((RET_HEADS, 1, 1))],
        out_specs=pl.BlockSpec((1, RET_CHUNK, c), lambda b, n: (b, n, 0)),
        out_shape=jax.ShapeDtypeStruct((batch, seq, c), out_dtype),
        scratch_shapes=[pltpu.VMEM((RET_HEADS, RET_DK, RET_DK), F32)],
        compiler_params=_cparams(("parallel", "arbitrary")),
        name="retention",
    )(proj3, proj3, proj3, proj3, cos_r, sin_r, dmat, xi, zeta, gch)


MERGE_TM = 256


def _merge_kernel(o0, o1, o2, l0, l1, l2, yb_ref, yc_ref, ga_ref, gb_ref, gc_ref, x_ref,
                  wba_ref, wbb_ref, wbc_ref, wo_ref, g_ref, b_ref, out_ref):
    e0, e1, e2 = l0[...], l1[...], l2[...]
    mx = jnp.maximum(jnp.maximum(e0, e1), e2)
    w0, w1, w2 = jnp.exp(e0 - mx), jnp.exp(e1 - mx), jnp.exp(e2 - mx)
    ya = (w0 * o0[...] + w1 * o1[...] + w2 * o2[...]) / (w0 + w1 + w2)
    za = jnp.dot(ya.astype(BF16), wba_ref[...], preferred_element_type=F32)
    zb = jnp.dot(yb_ref[...].astype(BF16), wbb_ref[...], preferred_element_type=F32)
    zc = jnp.dot(yc_ref[...].astype(BF16), wbc_ref[...], preferred_element_type=F32)
    merged = (jax.nn.sigmoid(ga_ref[...].astype(F32)) * za
              + jax.nn.sigmoid(gb_ref[...].astype(F32)) * zb
              + jax.nn.sigmoid(gc_ref[...].astype(F32)) * zc)
    mix = jnp.dot(merged.astype(BF16), wo_ref[...], preferred_element_type=F32)
    out_ref[...] = _layer_norm(np.float32(ALPHA) * x_ref[...] + mix, g_ref[...], b_ref[...])


def _merge(os_, ls_, yb, yc, proj, x, wba, wbb, wbc, wo, g, b):
    n = x.shape[0]
    c = D_MODEL
    tm = MERGE_TM
    row512 = pl.BlockSpec((tm, ATT_W), lambda i: (i, 0))
    row = lambda col: pl.BlockSpec((tm, c), lambda i: (i, col))
    full = lambda shape: pl.BlockSpec(shape, lambda i: (0, 0))
    return pl.pallas_call(
        _merge_kernel,
        grid=(n // tm,),
        in_specs=[row512] * 6 + [row(0), row(0), row(COL_GA // 2), row(COL_GB // 2), row(COL_GC // 2),
                                 row(0), full((ATT_W, c)), full((c, c)), full((c, c)), full((c, c)),
                                 full((1, c)), full((1, c))],
        out_specs=row(0),
        out_shape=jax.ShapeDtypeStruct((n, c), F32),
        compiler_params=_cparams(("parallel",)),
        name="merge_out_ln",
    )(*os_, *ls_, yb, yc, proj, proj, proj, x, wba, wbb, wbc, wo, g, b)


PEER_TQ = 256


def _top16(s):
    work = s
    vals = []
    for _ in range(PEER_TOPK):
        m = jnp.max(work, axis=0, keepdims=True)
        vals.append(m)
        work = jnp.where(work == m, NEG_INF, work)
    return vals, jnp.where(s >= vals[-1], s, NEG_INF)


def _peer_sel_kernel(x_ref, wq_ref, k1_ref, k2_ref, s1_ref, a_ref, s2_ref, bn_ref, tau_ref):
    q = jnp.dot(x_ref[...].astype(BF16), wq_ref[...], preferred_element_type=F32).astype(BF16)
    half = PEER_QDIM // 2
    nt = (((1,), (1,)), ((), ()))
    for h in range(PEER_HEADS):
        q1 = q[:, h * PEER_QDIM:h * PEER_QDIM + half]
        q2 = q[:, h * PEER_QDIM + half:(h + 1) * PEER_QDIM]
        s1 = lax.dot_general(k1_ref[h], q1, nt, preferred_element_type=F32)
        s2 = lax.dot_general(k2_ref[h], q2, nt, preferred_element_type=F32)
        v1, s1m = _top16(s1)
        v2, s2m = _top16(s2)
        v1s = jnp.concatenate(v1, axis=0)
        v2s = jnp.concatenate(v2, axis=0)
        cands = [v1s + v2[0]] + [v1s[0:8] + v2[b] for b in range(1, 8)] + [v1[0] + v2s[8:16]]
        cand = jnp.concatenate(cands, axis=0)
        work = cand
        for it in range(PEER_TOPK):
            tau = jnp.max(work, axis=0, keepdims=True)
            if it + 1 < PEER_TOPK:
                work = jnp.where(work == tau, NEG_INF, work)
        cmax = v1[0] + v2[0]
        z = jnp.sum(jnp.where(cand >= tau, jnp.exp(cand - cmax), 0.0), axis=0, keepdims=True)
        s1_ref[h] = s1m
        a_ref[h] = jnp.exp(s1 - v1[0])
        s2_ref[h] = s2m
        bn_ref[h] = jnp.exp(s2 - v2[0]) / z
        tau_ref[h] = tau


def _peer_select(x, wq, k1, k2):
    n = x.shape[0]
    tq = PEER_TQ
    hq = PEER_HEADS * PEER_QDIM
    big = pl.BlockSpec((PEER_HEADS, PEER_NKEYS, tq), lambda t: (0, 0, t))
    keys = pl.BlockSpec((PEER_HEADS, PEER_NKEYS, PEER_QDIM // 2), lambda t: (0, 0, 0))
    shp = jax.ShapeDtypeStruct((PEER_HEADS, PEER_NKEYS, n), F32)
    return pl.pallas_call(
        _peer_sel_kernel,
        grid=(n // tq,),
        in_specs=[pl.BlockSpec((tq, D_MODEL), lambda t: (t, 0)),
                  pl.BlockSpec((D_MODEL, hq), lambda t: (0, 0)), keys, keys],
        out_specs=[big, big, big, big, pl.BlockSpec((PEER_HEADS, 1, tq), lambda t: (0, 0, t))],
        out_shape=[shp, shp, shp, shp, jax.ShapeDtypeStruct((PEER_HEADS, 1, n), F32)],
        compiler_params=_cparams(("parallel",)),
        name="peer_select",
    )(x, wq, k1, k2)


PEER_T = 256
PEER_EC = 1024
PEER_IPC = PEER_EC // PEER_NKEYS


def _peer_kernel(x_ref, u_ref, vt_ref, s1_ref, a_ref, s2_ref, bn_ref, tau_ref, g_ref, b_ref,
                 o_ref, xb_ref, acc_ref, act_ref):
    e = pl.program_id(1)

    @pl.when(e == 0)
    def _():
        xb_ref[...] = x_ref[...].astype(BF16)
        acc_ref[...] = jnp.zeros_like(acc_ref)

    pre = lax.dot_general(u_ref[...], xb_ref[...], (((1,), (1,)), ((), ())),
                          preferred_element_type=F32)
    for ii in range(PEER_IPC):
        w = jnp.zeros((PEER_NKEYS, x_ref.shape[0]), F32)
        for h in range(PEER_HEADS):
            s1row = s1_ref[h, 0, ii:ii + 1, :]
            arow = a_ref[h, 0, ii:ii + 1, :]
            hit = (s2_ref[h] + s1row) >= tau_ref[h]
            w = w + jnp.where(hit, bn_ref[h], 0.0) * arow
        p = pre[ii * PEER_NKEYS:(ii + 1) * PEER_NKEYS]
        act_ref[ii * PEER_NKEYS:(ii + 1) * PEER_NKEYS, :] = (_gelu(p) * w).astype(BF16)
    acc_ref[...] += jnp.dot(vt_ref[...], act_ref[...], preferred_element_type=F32)

    @pl.when(e == pl.num_programs(1) - 1)
    def _():
        y = acc_ref[...].T
        o_ref[...] = _layer_norm(np.float32(ALPHA) * x_ref[...] + y, g_ref[...], b_ref[...])


def _peer_dense(x, u, vt, s1m, a, s2m, bn, tau, g, b):
    n = x.shape[0]
    t = PEER_T
    ne = u.shape[0] // PEER_EC
    s1v = s1m.reshape(PEER_HEADS, ne, PEER_IPC, n)
    av = a.reshape(PEER_HEADS, ne, PEER_IPC, n)
    rows = pl.BlockSpec((PEER_HEADS, 1, PEER_IPC, t), lambda i, e: (0, e, 0, i))
    big = pl.BlockSpec((PEER_HEADS, PEER_NKEYS, t), lambda i, e: (0, 0, i))
    vec = pl.BlockSpec((1, D_MODEL), lambda i, e: (0, 0))
    return pl.pallas_call(
        _peer_kernel,
        grid=(n // t, ne),
        in_specs=[pl.BlockSpec((t, D_MODEL), lambda i, e: (i, 0)),
                  pl.BlockSpec((PEER_EC, D_MODEL), lambda i, e: (e, 0)),
                  pl.BlockSpec((D_MODEL, PEER_EC), lambda i, e: (0, e)),
                  rows, rows, big, big,
                  pl.BlockSpec((PEER_HEADS, 1, t), lambda i, e: (0, 0, i)), vec, vec],
        out_specs=pl.BlockSpec((t, D_MODEL), lambda i, e: (i, 0)),
        out_shape=jax.ShapeDtypeStruct((n, D_MODEL), F32),
        scratch_shapes=[pltpu.VMEM((t, D_MODEL), BF16), pltpu.VMEM((D_MODEL, t), F32),
                        pltpu.VMEM((PEER_EC, t), BF16)],
        compiler_params=_cparams(("parallel", "arbitrary")),
        name="peer_dense",
    )(x, u, vt, s1v, av, s2m, bn, tau, g, b)


def _rope_tables(seq, dim):
    inv = ROPE_THETA ** (-jnp.arange(0, dim, 2, dtype=F32) / dim)
    ang = jnp.arange(seq, dtype=F32)[:, None] * inv[None, :]
    return jnp.cos(ang), jnp.sin(ang)


def _retention_tables():
    log_g = jnp.log(1.0 - 2.0 ** (-5.0 - jnp.arange(RET_HEADS, dtype=F32)))
    idx = jnp.arange(RET_CHUNK, dtype=F32)
    diff = idx[:, None] - idx[None, :]
    dmat = jnp.where(diff >= 0, jnp.exp(jnp.maximum(diff, 0.0)[None] * log_g[:, None, None]), 0.0)
    xi = jnp.exp((idx[None] + 1.0) * log_g[:, None])[..., None]
    zeta = jnp.exp((RET_CHUNK - 1.0 - idx[None]) * log_g[:, None])[..., None]
    gch = jnp.exp(RET_CHUNK * log_g)[:, None, None]
    return dmat, xi, zeta, gch


def _block_diag(w):
    g, i, o = w.shape
    eye = jnp.eye(g, dtype=w.dtype)
    return jnp.einsum('gio,gh->giho', w, eye).reshape(g * i, g * o)


def kernel(x, w_in, conv_w, conv_b, lru_wa, lru_ba, lru_wx, lru_bx, lru_lambda, w_branch, w_out,
           ln1_g, ln1_b, peer_wq, peer_k1, peer_k2, peer_u, peer_v, ln2_g, ln2_b):
    batch, seq, d = x.shape
    n = batch * seq
    depth = w_in.shape[0]
    act_dtype = F32

    cos_a, sin_a = _rope_tables(seq, ATT_E)
    cos_a2 = jnp.concatenate([cos_a, cos_a], axis=1)
    sin_a2 = jnp.concatenate([-sin_a, sin_a], axis=1)
    cos_r, sin_r = _rope_tables(seq, RET_DK)
    dmat, xi, zeta, gch = _retention_tables()

    xf = x.reshape(n, d)
    for l in range(depth):
        w_in_p = jnp.concatenate([w_in[l][:, 3 * 1536:], w_in[l][:, :3 * 1536]], axis=1).astype(BF16)
        wg = jnp.concatenate([_block_diag(lru_wa[l]), _block_diag(lru_wx[l])], axis=1).astype(BF16)
        bg = jnp.concatenate([lru_ba[l], lru_bx[l]])[None, :]
        wb = w_branch[l].astype(BF16)
        wba, wbb, wbc = wb[:ATT_W], wb[ATT_W:ATT_W + d], wb[ATT_W + d:]
        wo = w_out[l].astype(BF16)
        wq = peer_wq[l].astype(BF16)
        k1 = peer_k1[l].astype(BF16)
        k2 = peer_k2[l].astype(BF16)
        u = peer_u[l].astype(BF16)
        vt = peer_v[l].T.astype(BF16)

        proj = _matmul(xf, w_in_p, act_dtype)
        proj3 = proj.reshape(batch, seq, D_IN)
        os_, ls_ = [], []
        for g, (_, dil) in enumerate(ATT_GROUPS):
            o, lse = _attention_group(proj, cos_a2, sin_a2, g, dil, batch, seq)
            os_.append(o)
            ls_.append(lse)
        yb = _lru(proj3, conv_w[l], conv_b[l][None, :], wg, bg, lru_lambda[l][None, :], act_dtype)
        yc = _retention(proj3, cos_r, sin_r, dmat, xi, zeta, gch, act_dtype)
        x1 = _merge(os_, ls_, yb.reshape(n, d), yc.reshape(n, d), proj, xf, wba, wbb, wbc, wo,
                    ln1_g[l][None, :], ln1_b[l][None, :])
        s1m, a, s2m, bn, tau = _peer_select(x1, wq, k1, k2)
        xf = _peer_dense(x1, u, vt, s1m, a, s2m, bn, tau, ln2_g[l][None, :], ln2_b[l][None, :])
    return xf.reshape(batch, seq, d)
```

```python
import functools
import math

import numpy as np
import jax
import jax.numpy as jnp
from jax import lax
from jax.experimental import pallas as pl
from jax.experimental.pallas import tpu as pltpu

F32 = jnp.float32
BF16 = jnp.bfloat16

D_MODEL = 1024
DEPTH = 4
ROPE_THETA = 10000.0
LN_EPS = 1e-5
ATT_GROUPS = ((128, 1), (512, 4), (2048, 16))
ATT_HPG = 4
ATT_E = 128
ATT_BLK = 128
ATT_W = ATT_HPG * ATT_E
LRU_BLOCKS = 16
CONV_WIDTH = 4
LRU_C = 8.0
RET_HEADS = 4
RET_DK = 256
RET_CHUNK = 128
PEER_HEADS = 8
PEER_NKEYS = 128
PEER_QDIM = 256
PEER_TOPK = 16
ALPHA = (2 * DEPTH) ** 0.25

D_IN = 13824
COL_LX, COL_LG, COL_RQ, COL_RK, COL_RV, COL_RG, COL_GA, COL_GB, COL_GC = (2 * i for i in range(9))
COL_AQ, COL_AK, COL_AV = 18, 21, 24
NCOL512 = D_IN // 512

VMEM_LIMIT = 56 * 1024 * 1024
LANES = 128
NEG_INF = float("-inf")


def _cparams(sem):
    return pltpu.CompilerParams(dimension_semantics=sem, vmem_limit_bytes=VMEM_LIMIT)


def _layer_norm(z, g, b):
    mu = jnp.mean(z, axis=-1, keepdims=True)
    zc = z - mu
    var = jnp.mean(zc * zc, axis=-1, keepdims=True)
    return zc * lax.rsqrt(var + LN_EPS) * g + b


def _gelu(x):
    return 0.5 * x * (1.0 + lax.erf(x * np.float32(math.sqrt(0.5))))


def _mm_kernel(x_ref, w_ref, o_ref):
    o_ref[...] = jnp.dot(x_ref[...].astype(BF16), w_ref[...],
                         preferred_element_type=F32).astype(o_ref.dtype)


def _matmul(x, w, out_dtype, tm=512, tn=1536):
    n, k = x.shape
    m = w.shape[1]
    return pl.pallas_call(
        _mm_kernel,
        grid=(m // tn, n // tm),
        in_specs=[pl.BlockSpec((tm, k), lambda j, i: (i, 0)),
                  pl.BlockSpec((k, tn), lambda j, i: (0, j))],
        out_specs=pl.BlockSpec((tm, tn), lambda j, i: (i, j)),
        out_shape=jax.ShapeDtypeStruct((n, m), out_dtype),
        compiler_params=_cparams(("parallel", "parallel")),
        name="in_proj",
    )(x, w)


ATT_SPAN = 2048


def _att_kernel(dil, q_ref, kp_ref, kc_ref, vp_ref, vc_ref, cc_ref, sc_ref, cp_ref, sp_ref,
                o_ref, lse_ref, qs_ref, ks_ref, kps_ref):
    n = pl.program_id(2)
    halo = dil * ATT_BLK

    def rope(t, c, s):
        return t * c + pltpu.roll(t, ATT_E // 2, axis=1) * s

    cos_c, sin_c = cc_ref[...], sc_ref[...]
    qs_ref[...] = rope(q_ref[0].astype(F32), cos_c, sin_c)
    ks_ref[...] = rope(kc_ref[0].astype(F32), cos_c, sin_c)
    kps_ref[...] = rope(kp_ref[0, ATT_SPAN - halo:, :].astype(F32),
                        cp_ref[ATT_SPAN - halo:, :], sp_ref[ATT_SPAN - halo:, :])

    qi = lax.broadcasted_iota(jnp.int32, (ATT_BLK, 2 * ATT_BLK), 0)
    kj = lax.broadcasted_iota(jnp.int32, (ATT_BLK, 2 * ATT_BLK), 1)
    delta = qi + ATT_BLK - kj
    inside = jnp.where(delta >= 0, jnp.where(delta <= ATT_BLK, 1, 0), 0)
    has_prev = jnp.where(n > 0, 1, 0)
    valid_mid = inside > 0
    valid_first = jnp.where(kj >= ATT_BLK, inside, inside * has_prev) > 0

    def rows(start):
        return pl.ds(start, ATT_BLK, stride=dil) if dil > 1 else pl.ds(start, ATT_BLK)

    for r in range(dil):
        for jb in range(ATT_SPAN // halo):
            cur = rows(r + jb * halo)
            q = qs_ref[cur, :].astype(BF16)
            if jb == 0:
                prev = rows(ATT_SPAN - halo + r)
                kp, vp = kps_ref[rows(r), :], vp_ref[0, prev, :]
            else:
                prev = rows(r + (jb - 1) * halo)
                kp, vp = ks_ref[prev, :], vc_ref[0, prev, :]
            kcat = jnp.concatenate([kp, ks_ref[cur, :]], axis=0).astype(BF16)
            vcat = jnp.concatenate([vp, vc_ref[0, cur, :]], axis=0).astype(BF16)
            s = lax.dot_general(q, kcat, (((1,), (1,)), ((), ())), preferred_element_type=F32)
            s = jnp.where(valid_first if jb == 0 else valid_mid, s * np.float32(ATT_E ** -0.5), NEG_INF)
            m = jnp.max(s, axis=-1, keepdims=True)
            p = jnp.exp(s - m)
            l = jnp.sum(p, axis=-1, keepdims=True)
            o = jnp.dot(p.astype(BF16), vcat, preferred_element_type=F32)
            o_ref[0, cur, :] = (o / l).astype(o_ref.dtype)
            lse_ref[0, cur, :] = jnp.broadcast_to(m + jnp.log(l), (ATT_BLK, ATT_E))


def _attention_group(proj3, cos_t, sin_t, g, dil):
    batch, seq, _ = proj3.shape
    assert seq % ATT_SPAN == 0 and ATT_SPAN % (dil * ATT_BLK) == 0
    heads_col = lambda col: col * (512 // ATT_E) + g * ATT_HPG

    def cur(col):
        return pl.BlockSpec((1, ATT_SPAN, ATT_E), lambda b, h, n: (b, n, heads_col(col) + h))

    def prev(col):
        return pl.BlockSpec((1, ATT_SPAN, ATT_E),
                            lambda b, h, n: (b, jnp.maximum(n - 1, 0), heads_col(col) + h))

    tab_c = pl.BlockSpec((ATT_SPAN, ATT_E), lambda b, h, n: (n, 0))
    tab_p = pl.BlockSpec((ATT_SPAN, ATT_E), lambda b, h, n: (jnp.maximum(n - 1, 0), 0))
    out_spec = pl.BlockSpec((1, ATT_SPAN, ATT_E), lambda b, h, n: (b, n, h))
    o, lse = pl.pallas_call(
        functools.partial(_att_kernel, dil),
        grid=(batch, ATT_HPG, seq // ATT_SPAN),
        in_specs=[cur(COL_AQ), prev(COL_AK), cur(COL_AK), prev(COL_AV), cur(COL_AV),
                  tab_c, tab_c, tab_p, tab_p],
        out_specs=[out_spec, out_spec],
        out_shape=[jax.ShapeDtypeStruct((batch, seq, ATT_W), F32)] * 2,
        scratch_shapes=[pltpu.VMEM((ATT_SPAN, ATT_E), F32), pltpu.VMEM((ATT_SPAN, ATT_E), F32),
                        pltpu.VMEM((dil * ATT_BLK, ATT_E), F32)],
        compiler_params=_cparams(("parallel", "parallel", "parallel")),
        name=f"dilated_attention_d{dil}",
    )(proj3, proj3, proj3, proj3, proj3, cos_t, sin_t, cos_t, sin_t)
    return o.reshape(batch * seq, ATT_W), lse.reshape(batch * seq, ATT_W)


LRU_TS = 256
LRU_HALO = 8


def _lru_kernel(lx_ref, lg_ref, cw_ref, cb_ref, wg_ref, bg_ref, lam_ref, y_ref, ext_ref, h_ref):
    s_idx = pl.program_id(1)
    ts, c = lx_ref.shape[1], lx_ref.shape[2]

    @pl.when(s_idx == 0)
    def _():
        ext_ref[0:LRU_HALO, :] = jnp.zeros((LRU_HALO, c), F32)
        h_ref[...] = jnp.zeros_like(h_ref)

    ext_ref[LRU_HALO:, :] = lx_ref[0].astype(F32)
    xc = jnp.broadcast_to(cb_ref[...], (ts, c))
    for w in range(CONV_WIDTH):
        off = LRU_HALO - (CONV_WIDTH - 1) + w
        xc = xc + ext_ref[off:off + ts, :] * cw_ref[w:w + 1, :]
    ext_ref[0:LRU_HALO, :] = ext_ref[ts:ts + LRU_HALO, :]

    gates = jnp.dot(xc.astype(BF16), wg_ref[...], preferred_element_type=F32) + bg_ref[...]
    r = jax.nn.sigmoid(gates[:, :c])
    i = jax.nn.sigmoid(gates[:, c:])
    z = -lam_ref[...]
    softplus = jnp.maximum(z, 0.0) + jnp.log1p(jnp.exp(-jnp.abs(z)))
    log_a = (-LRU_C) * r * softplus
    a = jnp.exp(log_a)
    b = jnp.sqrt(1.0 - a * a) * (i * xc)

    row = lax.broadcasted_iota(jnp.int32, (ts, c), 0)
    k = 1
    while k < ts:
        keep = row >= k
        a_s = jnp.where(keep, pltpu.roll(a, k, axis=0), 1.0)
        b_s = jnp.where(keep, pltpu.roll(b, k, axis=0), 0.0)
        b = a * b_s + b
        a = a * a_s
        k *= 2
    h = b + a * h_ref[...]
    h_ref[...] = h[ts - 1:ts, :]
    y_ref[0] = (h * _gelu(lg_ref[0].astype(F32))).astype(y_ref.dtype)


def _lru(proj3, conv_w, conv_b, wg, bg, lam, out_dtype):
    batch, seq, _ = proj3.shape
    c = D_MODEL
    row = lambda b, s: (0, 0)
    return pl.pallas_call(
        _lru_kernel,
        grid=(batch, seq // LRU_TS),
        in_specs=[pl.BlockSpec((1, LRU_TS, c), lambda b, s: (b, s, COL_LX // 2)),
                  pl.BlockSpec((1, LRU_TS, c), lambda b, s: (b, s, COL_LG // 2)),
                  pl.BlockSpec((CONV_WIDTH, c), row),
                  pl.BlockSpec((1, c), row),
                  pl.BlockSpec((c, 2 * c), row),
                  pl.BlockSpec((1, 2 * c), row),
                  pl.BlockSpec((1, c), row)],
        out_specs=pl.BlockSpec((1, LRU_TS, c), lambda b, s: (b, s, 0)),
        out_shape=jax.ShapeDtypeStruct((batch, seq, c), out_dtype),
        scratch_shapes=[pltpu.VMEM((LRU_TS + LRU_HALO, c), F32), pltpu.VMEM((1, c), F32)],
        compiler_params=_cparams(("parallel", "arbitrary")),
        name="rg_lru",
    )(proj3, proj3, conv_w, conv_b, wg, bg, lam)


def _ret_kernel(q_ref, k_ref, v_ref, g_ref, cos_ref, sin_ref, dmat_ref, xi_ref, zeta_ref, gch_ref,
                y_ref, r_ref):
    @pl.when(pl.program_id(1) == 0)
    def _():
        r_ref[...] = jnp.zeros_like(r_ref)

    cos, sin = cos_ref[...], sin_ref[...]
    half = RET_DK // 2

    def rope(t):
        t1, t2 = t[:, :half], t[:, half:]
        return jnp.concatenate([t1 * cos - t2 * sin, t1 * sin + t2 * cos], axis=1)

    for h in range(RET_HEADS):
        sl = slice(h * RET_DK, (h + 1) * RET_DK)
        q = rope(q_ref[0, :, sl].astype(F32)).astype(BF16)
        k = rope(k_ref[0, :, sl].astype(F32)) * np.float32(RET_DK ** -0.5)
        v = v_ref[0, :, sl].astype(BF16)
        state = r_ref[h]
        qk = lax.dot_general(q, k.astype(BF16), (((1,), (1,)), ((), ())), preferred_element_type=F32)
        inner = jnp.dot((qk * dmat_ref[h]).astype(BF16), v, preferred_element_type=F32)
        cross = jnp.dot(q, state.astype(BF16), preferred_element_type=F32) * xi_ref[h]
        kz = (k * zeta_ref[h]).astype(BF16)
        r_ref[h] = gch_ref[h] * state + lax.dot_general(
            kz, v, (((0,), (0,)), ((), ())), preferred_element_type=F32)
        ret = inner + cross
        mu = jnp.mean(ret, axis=-1, keepdims=True)
        rc = ret - mu
        var = jnp.mean(rc * rc, axis=-1, keepdims=True)
        gate = g_ref[0, :, sl].astype(F32)
        y_ref[0, :, sl] = (gate * jax.nn.sigmoid(gate) * (rc * lax.rsqrt(var + LN_EPS))).astype(y_ref.dtype)


def _retention(proj3, cos_r, sin_r, dmat, xi, zeta, gch, out_dtype):
    batch, seq, _ = proj3.shape
    c = RET_HEADS * RET_DK
    blk = lambda col: pl.BlockSpec((1, RET_CHUNK, c), lambda b, n: (b, n, col // 2))
    tab = pl.BlockSpec((RET_CHUNK, RET_DK // 2), lambda b, n: (n, 0))
    const3 = lambda shape: pl.BlockSpec(shape, lambda b, n: (0, 0, 0))
    return pl.pallas_call(
        _ret_kernel,
        grid=(batch, seq // RET_CHUNK),
        in_specs=[blk(COL_RQ), blk(COL_RK), blk(COL_RV), blk(COL_RG), tab, tab,
                  const3((RET_HEADS, RET_CHUNK, RET_CHUNK)),
                  const3((RET_HEADS, RET_CHUNK, 1)),
                  const3((RET_HEADS, RET_CHUNK, 1)),
                  const3((RET_HEADS, 1, 1))],
        out_specs=pl.BlockSpec((1, RET_CHUNK, c), lambda b, n: (b, n, 0)),
        out_shape=jax.ShapeDtypeStruct((batch, seq, c), out_dtype),
        scratch_shapes=[pltpu.VMEM((RET_HEADS, RET_DK, RET_DK), F32)],
        compiler_params=_cparams(("parallel", "arbitrary")),
        name="retention",
    )(proj3, proj3, proj3, proj3, cos_r, sin_r, dmat, xi, zeta, gch)


MERGE_TM = 256


def _merge_kernel(o0, o1, o2, l0, l1, l2, yb_ref, yc_ref, ga_ref, gb_ref, gc_ref, x_ref,
                  wba_ref, wbb_ref, wbc_ref, wo_ref, g_ref, b_ref, out_ref):
    e0, e1, e2 = l0[...], l1[...], l2[...]
    mx = jnp.maximum(jnp.maximum(e0, e1), e2)
    w0, w1, w2 = jnp.exp(e0 - mx), jnp.exp(e1 - mx), jnp.exp(e2 - mx)
    ya = (w0 * o0[...] + w1 * o1[...] + w2 * o2[...]) / (w0 + w1 + w2)
    za = jnp.dot(ya.astype(BF16), wba_ref[...], preferred_element_type=F32)
    zb = jnp.dot(yb_ref[...].astype(BF16), wbb_ref[...], preferred_element_type=F32)
    zc = jnp.dot(yc_ref[...].astype(BF16), wbc_ref[...], preferred_element_type=F32)
    merged = (jax.nn.sigmoid(ga_ref[...].astype(F32)) * za
              + jax.nn.sigmoid(gb_ref[...].astype(F32)) * zb
              + jax.nn.sigmoid(gc_ref[...].astype(F32)) * zc)
    mix = jnp.dot(merged.astype(BF16), wo_ref[...], preferred_element_type=F32)
    out_ref[...] = _layer_norm(np.float32(ALPHA) * x_ref[...] + mix, g_ref[...], b_ref[...])


def _merge(os_, ls_, yb, yc, proj, x, wba, wbb, wbc, wo, g, b):
    n = x.shape[0]
    c = D_MODEL
    tm = MERGE_TM
    row512 = pl.BlockSpec((tm, ATT_W), lambda i: (i, 0))
    row = lambda col: pl.BlockSpec((tm, c), lambda i: (i, col))
    full = lambda shape: pl.BlockSpec(shape, lambda i: (0, 0))
    return pl.pallas_call(
        _merge_kernel,
        grid=(n // tm,),
        in_specs=[row512] * 6 + [row(0), row(0), row(COL_GA // 2), row(COL_GB // 2), row(COL_GC // 2),
                                 row(0), full((ATT_W, c)), full((c, c)), full((c, c)), full((c, c)),
                                 full((1, c)), full((1, c))],
        out_specs=row(0),
        out_shape=jax.ShapeDtypeStruct((n, c), F32),
        compiler_params=_cparams(("parallel",)),
        name="merge_out_ln",
    )(*os_, *ls_, yb, yc, proj, proj, proj, x, wba, wbb, wbc, wo, g, b)


PEER_TQ = 256
POS_INF = float("inf")


def _top16(s):
    work = s
    vals = []
    for _ in range(PEER_TOPK):
        m = jnp.max(work, axis=0, keepdims=True)
        vals.append(m)
        work = jnp.where(work == m, NEG_INF, work)
    return vals


def _peer_sel_kernel(x_ref, wq_ref, k1_ref, k2_ref, th_ref, a_ref, s2_ref, bn_ref):
    q = jnp.dot(x_ref[...].astype(BF16), wq_ref[...], preferred_element_type=F32).astype(BF16)
    half = PEER_QDIM // 2
    nt = (((1,), (1,)), ((), ()))
    for h in range(PEER_HEADS):
        q1 = q[:, h * PEER_QDIM:h * PEER_QDIM + half]
        q2 = q[:, h * PEER_QDIM + half:(h + 1) * PEER_QDIM]
        s1 = lax.dot_general(k1_ref[h], q1, nt, preferred_element_type=F32)
        s2 = lax.dot_general(k2_ref[h], q2, nt, preferred_element_type=F32)
        v1 = _top16(s1)
        v2 = _top16(s2)
        v1s = jnp.concatenate(v1, axis=0)
        v2s = jnp.concatenate(v2, axis=0)
        cands = [v1s + v2[0]] + [v1s[0:8] + v2[b] for b in range(1, 8)] + [v1[0] + v2s[8:16]]
        cand = jnp.concatenate(cands, axis=0)
        work = cand
        for it in range(PEER_TOPK):
            tau = jnp.max(work, axis=0, keepdims=True)
            if it + 1 < PEER_TOPK:
                work = jnp.where(work == tau, NEG_INF, work)
        cmax = v1[0] + v2[0]
        z = jnp.sum(jnp.where(cand >= tau, jnp.exp(cand - cmax), 0.0), axis=0, keepdims=True)
        th = jnp.full(s1.shape, POS_INF, F32)
        for a in range(PEER_TOPK):
            theta = jnp.min(jnp.where(v1[a] + v2s >= tau, v2s, POS_INF), axis=0, keepdims=True)
            th = jnp.where(s1 == v1[a], theta, th)
        th_ref[h] = th
        a_ref[h] = jnp.exp(s1 - v1[0])
        bn = jnp.exp(s2 - v2[0]) * (0.5 / z)
        for c in range(s2.shape[1] // LANES):
            s2_ref[h, c] = s2[:, c * LANES:(c + 1) * LANES]
            bn_ref[h, c] = bn[:, c * LANES:(c + 1) * LANES]


def _peer_select(x, wq, k1, k2):
    n = x.shape[0]
    tq = PEER_TQ
    hq = PEER_HEADS * PEER_QDIM
    big = pl.BlockSpec((PEER_HEADS, PEER_NKEYS, tq), lambda t: (0, 0, t))
    tiled = pl.BlockSpec((PEER_HEADS, tq // LANES, PEER_NKEYS, LANES), lambda t: (0, t, 0, 0))
    keys = pl.BlockSpec((PEER_HEADS, PEER_NKEYS, PEER_QDIM // 2), lambda t: (0, 0, 0))
    shp = jax.ShapeDtypeStruct((PEER_HEADS, PEER_NKEYS, n), F32)
    shp_tiled = jax.ShapeDtypeStruct((PEER_HEADS, n // LANES, PEER_NKEYS, LANES), F32)
    return pl.pallas_call(
        _peer_sel_kernel,
        grid=(n // tq,),
        in_specs=[pl.BlockSpec((tq, D_MODEL), lambda t: (t, 0)),
                  pl.BlockSpec((D_MODEL, hq), lambda t: (0, 0)), keys, keys],
        out_specs=[big, big, tiled, tiled],
        out_shape=[shp, shp, shp_tiled, shp_tiled],
        compiler_params=_cparams(("parallel",)),
        name="peer_select",
    )(x, wq, k1, k2)


PEER_T = 512
PEER_EC = 1024
PEER_IPC = PEER_EC // PEER_NKEYS


def _peer_kernel(x_ref, u_ref, vt_ref, th_ref, a_ref, s2_ref, bn_ref, g_ref, b_ref,
                 o_ref, xt_ref, acc_ref, pre0_ref, pre1_ref, act0_ref, act1_ref):
    e = pl.program_id(1)
    ne = pl.num_programs(1) - 2
    t = x_ref.shape[0]

    quads = [(slice(mh * 512, (mh + 1) * 512), slice(nh * 256, (nh + 1) * 256))
             for mh in range(2) for nh in range(t // 256)]

    def first_matmul(pre_ref):
        def piece(rows, cols):
            pre_ref[rows, cols] = jnp.dot(u_ref[rows, :], xt_ref[:, cols], preferred_element_type=F32)
        return [functools.partial(piece, r, c) for r, c in quads]

    def second_matmul(act_ref):
        def piece(rows, cols):
            acc_ref[rows, cols] += jnp.dot(vt_ref[rows, :], act_ref[:, cols], preferred_element_type=F32)
        return [functools.partial(piece, r, c) for r, c in quads]

    def gate(pre_ref, act_ref):
        def piece(ii, lt):
            rows = slice(ii * PEER_NKEYS, (ii + 1) * PEER_NKEYS)
            lanes = slice(lt * LANES, (lt + 1) * LANES)
            w = None
            for h in range(PEER_HEADS):
                hit = s2_ref[h, lt] >= th_ref[h, ii:ii + 1, lanes]
                term = jnp.where(hit, bn_ref[h, lt], 0.0) * a_ref[h, ii:ii + 1, lanes]
                w = term if w is None else w + term
            p = pre_ref[rows, lanes]
            gated = (p * w) * (1.0 + lax.erf(p * np.float32(math.sqrt(0.5))))
            act_ref[rows, lanes] = gated.astype(BF16)
        return [functools.partial(piece, ii, lt) for ii in range(PEER_IPC) for lt in range(t // LANES)]

    def interleave(mxu_pieces, vpu_pieces):
        per = -(-len(vpu_pieces) // max(len(mxu_pieces), 1)) if vpu_pieces else 0
        for k, mp in enumerate(mxu_pieces):
            mp()
            for vp in vpu_pieces[k * per:(k + 1) * per]:
                vp()
        for vp in vpu_pieces[len(mxu_pieces) * per:]:
            vp()

    @pl.when(e == 0)
    def _():
        xt_ref[...] = x_ref[...].T.astype(BF16)
        acc_ref[...] = jnp.zeros_like(acc_ref)
        interleave(first_matmul(pre0_ref), [])

    @pl.when(e == 1)
    def _():
        interleave(first_matmul(pre1_ref), gate(pre0_ref, act0_ref))

    @pl.when(jnp.logical_and(jnp.logical_and(e >= 2, e < ne), e % 2 == 0))
    def _():
        interleave(first_matmul(pre0_ref) + second_matmul(act0_ref), gate(pre1_ref, act1_ref))

    @pl.when(jnp.logical_and(jnp.logical_and(e >= 2, e < ne), e % 2 == 1))
    def _():
        interleave(first_matmul(pre1_ref) + second_matmul(act1_ref), gate(pre0_ref, act0_ref))

    @pl.when(e == ne)
    def _():
        interleave(second_matmul(act0_ref), gate(pre1_ref, act1_ref))

    @pl.when(e == ne + 1)
    def _():
        interleave(second_matmul(act1_ref), [])
        y = acc_ref[...].T
        o_ref[...] = _layer_norm(np.float32(ALPHA) * x_ref[...] + y, g_ref[...], b_ref[...])


def _peer_dense(x, u, vt, th, a, s2, bn, g, b):
    n = x.shape[0]
    t = PEER_T
    ne = u.shape[0] // PEER_EC
    assert ne % 2 == 0 and ne >= 2
    chunk = lambda e, lag: jnp.clip(e - lag, 0, ne - 1)
    rows = pl.BlockSpec((PEER_HEADS, PEER_IPC, t), lambda i, e: (0, chunk(e, 1), i))
    big = pl.BlockSpec((PEER_HEADS, t // LANES, PEER_NKEYS, LANES), lambda i, e: (0, i, 0, 0))
    vec = pl.BlockSpec((1, D_MODEL), lambda i, e: (0, 0))
    return pl.pallas_call(
        _peer_kernel,
        grid=(n // t, ne + 2),
        in_specs=[pl.BlockSpec((t, D_MODEL), lambda i, e: (i, 0)),
                  pl.BlockSpec((PEER_EC, D_MODEL), lambda i, e: (chunk(e, 0), 0)),
                  pl.BlockSpec((D_MODEL, PEER_EC), lambda i, e: (0, chunk(e, 2))),
                  rows, rows, big, big, vec, vec],
        out_specs=pl.BlockSpec((t, D_MODEL), lambda i, e: (i, 0)),
        out_shape=jax.ShapeDtypeStruct((n, D_MODEL), F32),
        scratch_shapes=[pltpu.VMEM((D_MODEL, t), BF16), pltpu.VMEM((D_MODEL, t), F32),
                        pltpu.VMEM((PEER_EC, t), F32), pltpu.VMEM((PEER_EC, t), F32),
                        pltpu.VMEM((PEER_EC, t), BF16), pltpu.VMEM((PEER_EC, t), BF16)],
        compiler_params=_cparams(("parallel", "arbitrary")),
        name="peer_dense",
    )(x, u, vt, th, a, s2, bn, g, b)


def _rope_tables(seq, dim):
    inv = ROPE_THETA ** (-jnp.arange(0, dim, 2, dtype=F32) / dim)
    ang = jnp.arange(seq, dtype=F32)[:, None] * inv[None, :]
    return jnp.cos(ang), jnp.sin(ang)


def _retention_tables():
    log_g = jnp.log(1.0 - 2.0 ** (-5.0 - jnp.arange(RET_HEADS, dtype=F32)))
    idx = jnp.arange(RET_CHUNK, dtype=F32)
    diff = idx[:, None] - idx[None, :]
    dmat = jnp.where(diff >= 0, jnp.exp(jnp.maximum(diff, 0.0)[None] * log_g[:, None, None]), 0.0)
    xi = jnp.exp((idx[None] + 1.0) * log_g[:, None])[..., None]
    zeta = jnp.exp((RET_CHUNK - 1.0 - idx[None]) * log_g[:, None])[..., None]
    gch = jnp.exp(RET_CHUNK * log_g)[:, None, None]
    return dmat, xi, zeta, gch


def _block_diag(w):
    g, i, o = w.shape
    eye = jnp.eye(g, dtype=w.dtype)
    return jnp.einsum('gio,gh->giho', w, eye).reshape(g * i, g * o)


def kernel(x, w_in, conv_w, conv_b, lru_wa, lru_ba, lru_wx, lru_bx, lru_lambda, w_branch, w_out,
           ln1_g, ln1_b, peer_wq, peer_k1, peer_k2, peer_u, peer_v, ln2_g, ln2_b):
    batch, seq, d = x.shape
    n = batch * seq
    depth = w_in.shape[0]
    act_dtype = F32

    cos_a, sin_a = _rope_tables(seq, ATT_E)
    cos_a2 = jnp.concatenate([cos_a, cos_a], axis=1)
    sin_a2 = jnp.concatenate([-sin_a, sin_a], axis=1)
    cos_r, sin_r = _rope_tables(seq, RET_DK)
    dmat, xi, zeta, gch = _retention_tables()

    xf = x.reshape(n, d)
    for l in range(depth):
        w_in_p = jnp.concatenate([w_in[l][:, 3 * 1536:], w_in[l][:, :3 * 1536]], axis=1).astype(BF16)
        wg = jnp.concatenate([_block_diag(lru_wa[l]), _block_diag(lru_wx[l])], axis=1).astype(BF16)
        bg = jnp.concatenate([lru_ba[l], lru_bx[l]])[None, :]
        wb = w_branch[l].astype(BF16)
        wba, wbb, wbc = wb[:ATT_W], wb[ATT_W:ATT_W + d], wb[ATT_W + d:]
        wo = w_out[l].astype(BF16)
        wq = peer_wq[l].astype(BF16)
        k1 = peer_k1[l].astype(BF16)
        k2 = peer_k2[l].astype(BF16)
        u = peer_u[l].astype(BF16)
        vt = peer_v[l].T.astype(BF16)

        proj = _matmul(xf, w_in_p, act_dtype)
        proj3 = proj.reshape(batch, seq, D_IN)
        os_, ls_ = [], []
        for g, (_, dil) in enumerate(ATT_GROUPS):
            o, lse = _attention_group(proj3, cos_a2, sin_a2, g, dil)
            os_.append(o)
            ls_.append(lse)
        yb = _lru(proj3, conv_w[l], conv_b[l][None, :], wg, bg, lru_lambda[l][None, :], act_dtype)
        yc = _retention(proj3, cos_r, sin_r, dmat, xi, zeta, gch, act_dtype)
        x1 = _merge(os_, ls_, yb.reshape(n, d), yc.reshape(n, d), proj, xf, wba, wbb, wbc, wo,
                    ln1_g[l][None, :], ln1_b[l][None, :])
        th, a, s2, bn = _peer_select(x1, wq, k1, k2)
        xf = _peer_dense(x1, u, vt, th, a, s2, bn, ln2_g[l][None, :], ln2_b[l][None, :])
    return xf.reshape(batch, seq, d)
```

```python
import functools
import math

import numpy as np
import jax
import jax.numpy as jnp
from jax import lax
from jax.experimental import pallas as pl
from jax.experimental.pallas import tpu as pltpu

F32 = jnp.float32
BF16 = jnp.bfloat16

D_MODEL = 1024
DEPTH = 4
ROPE_THETA = 10000.0
LN_EPS = 1e-5
ATT_GROUPS = ((128, 1), (512, 4), (2048, 16))
ATT_HPG = 4
ATT_E = 128
ATT_BLK = 128
ATT_W = ATT_HPG * ATT_E
LRU_BLOCKS = 16
CONV_WIDTH = 4
LRU_C = 8.0
RET_HEADS = 4
RET_DK = 256
RET_CHUNK = 128
PEER_HEADS = 8
PEER_NKEYS = 128
PEER_QDIM = 256
PEER_TOPK = 16
ALPHA = (2 * DEPTH) ** 0.25

D_ATT = 3 * 1536
D_REST = 9 * 1024
COL_LX, COL_LG, COL_RQ, COL_RK, COL_RV, COL_RG, COL_GA, COL_GB, COL_GC = (2 * i for i in range(9))
COL_AQ, COL_AK, COL_AV = 0, 3, 6

VMEM_LIMIT = 56 * 1024 * 1024
LANES = 128
NEG_INF = float("-inf")


def _cparams(sem):
    return pltpu.CompilerParams(dimension_semantics=sem, vmem_limit_bytes=VMEM_LIMIT)


def _layer_norm(z, g, b):
    mu = jnp.mean(z, axis=-1, keepdims=True)
    zc = z - mu
    var = jnp.mean(zc * zc, axis=-1, keepdims=True)
    return zc * lax.rsqrt(var + LN_EPS) * g + b


def _gelu(x):
    return 0.5 * x * (1.0 + lax.erf(x * np.float32(math.sqrt(0.5))))


def _mm_kernel(x_ref, w_ref, o_ref):
    o_ref[...] = jnp.dot(x_ref[...], w_ref[...], preferred_element_type=F32).astype(o_ref.dtype)


def _matmul(x, w, out_dtype, tm=512, tn=1536):
    n, k = x.shape
    m = w.shape[1]
    return pl.pallas_call(
        _mm_kernel,
        grid=(m // tn, n // tm),
        in_specs=[pl.BlockSpec((tm, k), lambda j, i: (i, 0)),
                  pl.BlockSpec((k, tn), lambda j, i: (0, j))],
        out_specs=pl.BlockSpec((tm, tn), lambda j, i: (i, j)),
        out_shape=jax.ShapeDtypeStruct((n, m), out_dtype),
        compiler_params=_cparams(("parallel", "parallel")),
        name="in_proj",
    )(x, w)


ATT_SPAN = 2048


def _att_kernel(dil, q_ref, kp_ref, kc_ref, vp_ref, vc_ref, cc_ref, sc_ref, cp_ref, sp_ref,
                o_ref, lse_ref, qs_ref, ks_ref, kps_ref):
    n = pl.program_id(2)
    halo = dil * ATT_BLK

    def rope(t, c, s):
        return t * c + pltpu.roll(t, ATT_E // 2, axis=1) * s

    cos_c, sin_c = cc_ref[...], sc_ref[...]
    qs_ref[...] = rope(q_ref[0].astype(F32), cos_c, sin_c)
    ks_ref[...] = rope(kc_ref[0].astype(F32), cos_c, sin_c)
    kps_ref[...] = rope(kp_ref[0, ATT_SPAN - halo:, :].astype(F32),
                        cp_ref[ATT_SPAN - halo:, :], sp_ref[ATT_SPAN - halo:, :])

    qi = lax.broadcasted_iota(jnp.int32, (ATT_BLK, 2 * ATT_BLK), 0)
    kj = lax.broadcasted_iota(jnp.int32, (ATT_BLK, 2 * ATT_BLK), 1)
    delta = qi + ATT_BLK - kj
    inside = jnp.where(delta >= 0, jnp.where(delta <= ATT_BLK, 1, 0), 0)
    has_prev = jnp.where(n > 0, 1, 0)
    valid_mid = inside > 0
    valid_first = jnp.where(kj >= ATT_BLK, inside, inside * has_prev) > 0

    def rows(start):
        return pl.ds(start, ATT_BLK, stride=dil) if dil > 1 else pl.ds(start, ATT_BLK)

    for r in range(dil):
        for jb in range(ATT_SPAN // halo):
            cur = rows(r + jb * halo)
            q = qs_ref[cur, :].astype(BF16)
            if jb == 0:
                prev = rows(ATT_SPAN - halo + r)
                kp, vp = kps_ref[rows(r), :], vp_ref[0, prev, :]
            else:
                prev = rows(r + (jb - 1) * halo)
                kp, vp = ks_ref[prev, :], vc_ref[0, prev, :]
            kcat = jnp.concatenate([kp, ks_ref[cur, :]], axis=0).astype(BF16)
            vcat = jnp.concatenate([vp, vc_ref[0, cur, :]], axis=0).astype(BF16)
            s = lax.dot_general(q, kcat, (((1,), (1,)), ((), ())), preferred_element_type=F32)
            s = jnp.where(valid_first if jb == 0 else valid_mid, s * np.float32(ATT_E ** -0.5), NEG_INF)
            m = jnp.max(s, axis=-1, keepdims=True)
            p = jnp.exp(s - m)
            l = jnp.sum(p, axis=-1, keepdims=True)
            o = jnp.dot(p.astype(BF16), vcat, preferred_element_type=F32)
            o_ref[0, cur, :] = (o / l).astype(o_ref.dtype)
            lse_ref[0, cur, :] = jnp.broadcast_to(m + jnp.log(l), (ATT_BLK, ATT_E))


def _attention_group(proj3, cos_t, sin_t, g, dil):
    batch, seq, _ = proj3.shape
    assert seq % ATT_SPAN == 0 and ATT_SPAN % (dil * ATT_BLK) == 0
    heads_col = lambda col: col * (512 // ATT_E) + g * ATT_HPG

    def cur(col):
        return pl.BlockSpec((1, ATT_SPAN, ATT_E), lambda b, h, n: (b, n, heads_col(col) + h))

    def prev(col):
        return pl.BlockSpec((1, ATT_SPAN, ATT_E),
                            lambda b, h, n: (b, jnp.maximum(n - 1, 0), heads_col(col) + h))

    tab_c = pl.BlockSpec((ATT_SPAN, ATT_E), lambda b, h, n: (n, 0))
    tab_p = pl.BlockSpec((ATT_SPAN, ATT_E), lambda b, h, n: (jnp.maximum(n - 1, 0), 0))
    out_spec = pl.BlockSpec((1, ATT_SPAN, ATT_E), lambda b, h, n: (b, n, h))
    o, lse = pl.pallas_call(
        functools.partial(_att_kernel, dil),
        grid=(batch, ATT_HPG, seq // ATT_SPAN),
        in_specs=[cur(COL_AQ), prev(COL_AK), cur(COL_AK), prev(COL_AV), cur(COL_AV),
                  tab_c, tab_c, tab_p, tab_p],
        out_specs=[out_spec, out_spec],
        out_shape=[jax.ShapeDtypeStruct((batch, seq, ATT_W), F32)] * 2,
        scratch_shapes=[pltpu.VMEM((ATT_SPAN, ATT_E), F32), pltpu.VMEM((ATT_SPAN, ATT_E), F32),
                        pltpu.VMEM((dil * ATT_BLK, ATT_E), F32)],
        compiler_params=_cparams(("parallel", "parallel", "parallel")),
        name=f"dilated_attention_d{dil}",
    )(proj3, proj3, proj3, proj3, proj3, cos_t, sin_t, cos_t, sin_t)
    return o.reshape(batch * seq, ATT_W), lse.reshape(batch * seq, ATT_W)


LRU_TS = 256
LRU_HALO = 8


def _lru_kernel(lx_ref, lg_ref, cw_ref, cb_ref, wg_ref, bg_ref, lam_ref, y_ref, ext_ref, h_ref):
    s_idx = pl.program_id(1)
    ts, c = lx_ref.shape[1], lx_ref.shape[2]

    @pl.when(s_idx == 0)
    def _():
        ext_ref[0:LRU_HALO, :] = jnp.zeros((LRU_HALO, c), F32)
        h_ref[...] = jnp.zeros_like(h_ref)

    ext_ref[LRU_HALO:, :] = lx_ref[0].astype(F32)
    xc = jnp.broadcast_to(cb_ref[...], (ts, c))
    for w in range(CONV_WIDTH):
        off = LRU_HALO - (CONV_WIDTH - 1) + w
        xc = xc + ext_ref[off:off + ts, :] * cw_ref[w:w + 1, :]
    ext_ref[0:LRU_HALO, :] = ext_ref[ts:ts + LRU_HALO, :]

    gates = jnp.dot(xc.astype(BF16), wg_ref[...], preferred_element_type=F32) + bg_ref[...]
    r = jax.nn.sigmoid(gates[:, :c])
    i = jax.nn.sigmoid(gates[:, c:])
    z = -lam_ref[...]
    softplus = jnp.maximum(z, 0.0) + jnp.log1p(jnp.exp(-jnp.abs(z)))
    log_a = (-LRU_C) * r * softplus
    a = jnp.exp(log_a)
    b = jnp.sqrt(1.0 - a * a) * (i * xc)

    row = lax.broadcasted_iota(jnp.int32, (ts, c), 0)
    k = 1
    while k < ts:
        keep = row >= k
        a_s = jnp.where(keep, pltpu.roll(a, k, axis=0), 1.0)
        b_s = jnp.where(keep, pltpu.roll(b, k, axis=0), 0.0)
        b = a * b_s + b
        a = a * a_s
        k *= 2
    h = b + a * h_ref[...]
    h_ref[...] = h[ts - 1:ts, :]
    y_ref[0] = (h * _gelu(lg_ref[0].astype(F32))).astype(y_ref.dtype)


def _lru(proj3, conv_w, conv_b, wg, bg, lam, out_dtype):
    batch, seq, _ = proj3.shape
    c = D_MODEL
    row = lambda b, s: (0, 0)
    return pl.pallas_call(
        _lru_kernel,
        grid=(batch, seq // LRU_TS),
        in_specs=[pl.BlockSpec((1, LRU_TS, c), lambda b, s: (b, s, COL_LX // 2)),
                  pl.BlockSpec((1, LRU_TS, c), lambda b, s: (b, s, COL_LG // 2)),
                  pl.BlockSpec((CONV_WIDTH, c), row),
                  pl.BlockSpec((1, c), row),
                  pl.BlockSpec((c, 2 * c), row),
                  pl.BlockSpec((1, 2 * c), row),
                  pl.BlockSpec((1, c), row)],
        out_specs=pl.BlockSpec((1, LRU_TS, c), lambda b, s: (b, s, 0)),
        out_shape=jax.ShapeDtypeStruct((batch, seq, c), out_dtype),
        scratch_shapes=[pltpu.VMEM((LRU_TS + LRU_HALO, c), F32), pltpu.VMEM((1, c), F32)],
        compiler_params=_cparams(("parallel", "arbitrary")),
        name="rg_lru",
    )(proj3, proj3, conv_w, conv_b, wg, bg, lam)


def _ret_kernel(q_ref, k_ref, v_ref, g_ref, cos_ref, sin_ref, dmat_ref, xi_ref, zeta_ref, gch_ref,
                y_ref, r_ref):
    @pl.when(pl.program_id(1) == 0)
    def _():
        r_ref[...] = jnp.zeros_like(r_ref)

    cos, sin = cos_ref[...], sin_ref[...]
    half = RET_DK // 2

    def rope(t):
        t1, t2 = t[:, :half], t[:, half:]
        return jnp.concatenate([t1 * cos - t2 * sin, t1 * sin + t2 * cos], axis=1)

    for h in range(RET_HEADS):
        sl = slice(h * RET_DK, (h + 1) * RET_DK)
        q = rope(q_ref[0, :, sl].astype(F32)).astype(BF16)
        k = rope(k_ref[0, :, sl].astype(F32)) * np.float32(RET_DK ** -0.5)
        v = v_ref[0, :, sl].astype(BF16)
        state = r_ref[h]
        qk = lax.dot_general(q, k.astype(BF16), (((1,), (1,)), ((), ())), preferred_element_type=F32)
        inner = jnp.dot((qk * dmat_ref[h]).astype(BF16), v, preferred_element_type=F32)
        cross = jnp.dot(q, state.astype(BF16), preferred_element_type=F32) * xi_ref[h]
        kz = (k * zeta_ref[h]).astype(BF16)
        r_ref[h] = gch_ref[h] * state + lax.dot_general(
            kz, v, (((0,), (0,)), ((), ())), preferred_element_type=F32)
        ret = inner + cross
        mu = jnp.mean(ret, axis=-1, keepdims=True)
        rc = ret - mu
        var = jnp.mean(rc * rc, axis=-1, keepdims=True)
        gate = g_ref[0, :, sl].astype(F32)
        y_ref[0, :, sl] = (gate * jax.nn.sigmoid(gate) * (rc * lax.rsqrt(var + LN_EPS))).astype(y_ref.dtype)


def _retention(proj3, cos_r, sin_r, dmat, xi, zeta, gch, out_dtype):
    batch, seq, _ = proj3.shape
    c = RET_HEADS * RET_DK
    blk = lambda col: pl.BlockSpec((1, RET_CHUNK, c), lambda b, n: (b, n, col // 2))
    tab = pl.BlockSpec((RET_CHUNK, RET_DK // 2), lambda b, n: (n, 0))
    const3 = lambda shape: pl.BlockSpec(shape, lambda b, n: (0, 0, 0))
    return pl.pallas_call(
        _ret_kernel,
        grid=(batch, seq // RET_CHUNK),
        in_specs=[blk(COL_RQ), blk(COL_RK), blk(COL_RV), blk(COL_RG), tab, tab,
                  const3((RET_HEADS, RET_CHUNK, RET_CHUNK)),
                  const3((RET_HEADS, RET_CHUNK, 1)),
                  const3((RET_HEADS, RET_CHUNK, 1)),
                  const3((RET_HEADS, 1, 1))],
        out_specs=pl.BlockSpec((1, RET_CHUNK, c), lambda b, n: (b, n, 0)),
        out_shape=jax.ShapeDtypeStruct((batch, seq, c), out_dtype),
        scratch_shapes=[pltpu.VMEM((RET_HEADS, RET_DK, RET_DK), F32)],
        compiler_params=_cparams(("parallel", "arbitrary")),
        name="retention",
    )(proj3, proj3, proj3, proj3, cos_r, sin_r, dmat, xi, zeta, gch)


MERGE_TM = 256


def _merge_kernel(o0, o1, o2, l0, l1, l2, yb_ref, yc_ref, ga_ref, gb_ref, gc_ref, x_ref,
                  wba_ref, wbb_ref, wbc_ref, wo_ref, g_ref, b_ref, out_ref):
    e0, e1, e2 = l0[...], l1[...], l2[...]
    mx = jnp.maximum(jnp.maximum(e0, e1), e2)
    w0, w1, w2 = jnp.exp(e0 - mx), jnp.exp(e1 - mx), jnp.exp(e2 - mx)
    ya = (w0 * o0[...] + w1 * o1[...] + w2 * o2[...]) / (w0 + w1 + w2)
    za = jnp.dot(ya.astype(BF16), wba_ref[...], preferred_element_type=F32)
    zb = jnp.dot(yb_ref[...], wbb_ref[...], preferred_element_type=F32)
    zc = jnp.dot(yc_ref[...], wbc_ref[...], preferred_element_type=F32)
    merged = (jax.nn.sigmoid(ga_ref[...].astype(F32)) * za
              + jax.nn.sigmoid(gb_ref[...].astype(F32)) * zb
              + jax.nn.sigmoid(gc_ref[...].astype(F32)) * zc)
    mix = jnp.dot(merged.astype(BF16), wo_ref[...], preferred_element_type=F32)
    out_ref[...] = _layer_norm(np.float32(ALPHA) * x_ref[...] + mix, g_ref[...], b_ref[...])


def _merge(os_, ls_, yb, yc, proj, x, wba, wbb, wbc, wo, g, b):
    n = x.shape[0]
    c = D_MODEL
    tm = MERGE_TM
    row512 = pl.BlockSpec((tm, ATT_W), lambda i: (i, 0))
    row = lambda col: pl.BlockSpec((tm, c), lambda i: (i, col))
    full = lambda shape: pl.BlockSpec(shape, lambda i: (0, 0))
    return pl.pallas_call(
        _merge_kernel,
        grid=(n // tm,),
        in_specs=[row512] * 6 + [row(0), row(0), row(COL_GA // 2), row(COL_GB // 2), row(COL_GC // 2),
                                 row(0), full((ATT_W, c)), full((c, c)), full((c, c)), full((c, c)),
                                 full((1, c)), full((1, c))],
        out_specs=row(0),
        out_shape=jax.ShapeDtypeStruct((n, c), F32),
        compiler_params=_cparams(("parallel",)),
        name="merge_out_ln",
    )(*os_, *ls_, yb, yc, proj, proj, proj, x, wba, wbb, wbc, wo, g, b)


PEER_TQ = 256
POS_INF = float("inf")


def _top16(s):
    work = s
    vals = []
    for _ in range(PEER_TOPK):
        m = jnp.max(work, axis=0, keepdims=True)
        vals.append(m)
        work = jnp.where(work == m, NEG_INF, work)
    return vals


def _peer_sel_kernel(x_ref, wq_ref, k1_ref, k2_ref, th_ref, a_ref, s2_ref, bn_ref):
    q = jnp.dot(x_ref[...].astype(BF16), wq_ref[...], preferred_element_type=F32).astype(BF16)
    half = PEER_QDIM // 2
    nt = (((1,), (1,)), ((), ()))
    for h in range(PEER_HEADS):
        q1 = q[:, h * PEER_QDIM:h * PEER_QDIM + half]
        q2 = q[:, h * PEER_QDIM + half:(h + 1) * PEER_QDIM]
        s1 = lax.dot_general(k1_ref[h], q1, nt, preferred_element_type=F32)
        s2 = lax.dot_general(k2_ref[h], q2, nt, preferred_element_type=F32)
        v1 = _top16(s1)
        v2 = _top16(s2)
        v1s = jnp.concatenate(v1, axis=0)
        v2s = jnp.concatenate(v2, axis=0)
        cands = [v1s + v2[0]] + [v1s[0:8] + v2[b] for b in range(1, 8)] + [v1[0] + v2s[8:16]]
        cand = jnp.concatenate(cands, axis=0)
        work = cand
        for it in range(PEER_TOPK):
            tau = jnp.max(work, axis=0, keepdims=True)
            if it + 1 < PEER_TOPK:
                work = jnp.where(work == tau, NEG_INF, work)
        cmax = v1[0] + v2[0]
        z = jnp.sum(jnp.where(cand >= tau, jnp.exp(cand - cmax), 0.0), axis=0, keepdims=True)
        th = jnp.full(s1.shape, POS_INF, F32)
        for a in range(PEER_TOPK):
            theta = jnp.min(jnp.where(v1[a] + v2s >= tau, v2s, POS_INF), axis=0, keepdims=True)
            th = jnp.where(s1 == v1[a], theta, th)
        th_ref[h] = th
        a_ref[h] = jnp.exp(s1 - v1[0])
        bn = jnp.exp(s2 - v2[0]) * (0.5 / z)
        for c in range(s2.shape[1] // LANES):
            s2_ref[h, c] = s2[:, c * LANES:(c + 1) * LANES]
            bn_ref[h, c] = bn[:, c * LANES:(c + 1) * LANES]


def _peer_select(x, wq, k1, k2):
    n = x.shape[0]
    tq = PEER_TQ
    hq = PEER_HEADS * PEER_QDIM
    big = pl.BlockSpec((PEER_HEADS, PEER_NKEYS, tq), lambda t: (0, 0, t))
    tiled = pl.BlockSpec((PEER_HEADS, tq // LANES, PEER_NKEYS, LANES), lambda t: (0, t, 0, 0))
    keys = pl.BlockSpec((PEER_HEADS, PEER_NKEYS, PEER_QDIM // 2), lambda t: (0, 0, 0))
    shp = jax.ShapeDtypeStruct((PEER_HEADS, PEER_NKEYS, n), F32)
    shp_tiled = jax.ShapeDtypeStruct((PEER_HEADS, n // LANES, PEER_NKEYS, LANES), F32)
    return pl.pallas_call(
        _peer_sel_kernel,
        grid=(n // tq,),
        in_specs=[pl.BlockSpec((tq, D_MODEL), lambda t: (t, 0)),
                  pl.BlockSpec((D_MODEL, hq), lambda t: (0, 0)), keys, keys],
        out_specs=[big, big, tiled, tiled],
        out_shape=[shp, shp, shp_tiled, shp_tiled],
        compiler_params=_cparams(("parallel",)),
        name="peer_select",
    )(x, wq, k1, k2)


PEER_T = 512
PEER_EC = 1024
PEER_IPC = PEER_EC // PEER_NKEYS


def _peer_kernel(x_ref, u_ref, vt_ref, th_ref, a_ref, s2_ref, bn_ref, g_ref, b_ref,
                 o_ref, ob_ref, xt_ref, acc_ref, pre0_ref, pre1_ref, act0_ref, act1_ref):
    e = pl.program_id(1)
    ne = pl.num_programs(1) - 2
    t = x_ref.shape[0]

    quads = [(slice(mh * 512, (mh + 1) * 512), slice(nh * 256, (nh + 1) * 256))
             for mh in range(2) for nh in range(t // 256)]

    def first_matmul(pre_ref):
        def piece(rows, cols):
            pre_ref[rows, cols] = jnp.dot(u_ref[rows, :], xt_ref[:, cols], preferred_element_type=F32)
        return [functools.partial(piece, r, c) for r, c in quads]

    def second_matmul(act_ref):
        def piece(rows, cols):
            acc_ref[rows, cols] += jnp.dot(vt_ref[rows, :], act_ref[:, cols], preferred_element_type=F32)
        return [functools.partial(piece, r, c) for r, c in quads]

    def gate(pre_ref, act_ref):
        def piece(ii, lt):
            rows = slice(ii * PEER_NKEYS, (ii + 1) * PEER_NKEYS)
            lanes = slice(lt * LANES, (lt + 1) * LANES)
            w = None
            for h in range(PEER_HEADS):
                hit = s2_ref[h, lt] >= th_ref[h, ii:ii + 1, lanes]
                term = jnp.where(hit, bn_ref[h, lt], 0.0) * a_ref[h, ii:ii + 1, lanes]
                w = term if w is None else w + term
            p = pre_ref[rows, lanes]
            gated = (p * w) * (1.0 + lax.erf(p * np.float32(math.sqrt(0.5))))
            act_ref[rows, lanes] = gated.astype(BF16)
        return [functools.partial(piece, ii, lt) for ii in range(PEER_IPC) for lt in range(t // LANES)]

    def interleave(mxu_pieces, vpu_pieces):
        per = -(-len(vpu_pieces) // max(len(mxu_pieces), 1)) if vpu_pieces else 0
        for k, mp in enumerate(mxu_pieces):
            mp()
            for vp in vpu_pieces[k * per:(k + 1) * per]:
                vp()
        for vp in vpu_pieces[len(mxu_pieces) * per:]:
            vp()

    @pl.when(e == 0)
    def _():
        xt_ref[...] = x_ref[...].T.astype(BF16)
        acc_ref[...] = jnp.zeros_like(acc_ref)
        interleave(first_matmul(pre0_ref), [])

    @pl.when(e == 1)
    def _():
        interleave(first_matmul(pre1_ref), gate(pre0_ref, act0_ref))

    @pl.when(jnp.logical_and(jnp.logical_and(e >= 2, e < ne), e % 2 == 0))
    def _():
        interleave(first_matmul(pre0_ref) + second_matmul(act0_ref), gate(pre1_ref, act1_ref))

    @pl.when(jnp.logical_and(jnp.logical_and(e >= 2, e < ne), e % 2 == 1))
    def _():
        interleave(first_matmul(pre1_ref) + second_matmul(act1_ref), gate(pre0_ref, act0_ref))

    @pl.when(e == ne)
    def _():
        interleave(second_matmul(act0_ref), gate(pre1_ref, act1_ref))

    @pl.when(e == ne + 1)
    def _():
        interleave(second_matmul(act1_ref), [])
        y = acc_ref[...].T
        out = _layer_norm(np.float32(ALPHA) * x_ref[...] + y, g_ref[...], b_ref[...])
        o_ref[...] = out
        ob_ref[...] = out.astype(BF16)


def _peer_dense(x, u, vt, th, a, s2, bn, g, b):
    n = x.shape[0]
    t = PEER_T
    ne = u.shape[0] // PEER_EC
    assert ne % 2 == 0 and ne >= 2
    chunk = lambda e, lag: jnp.clip(e - lag, 0, ne - 1)
    rows = pl.BlockSpec((PEER_HEADS, PEER_IPC, t), lambda i, e: (0, chunk(e, 1), i))
    big = pl.BlockSpec((PEER_HEADS, t // LANES, PEER_NKEYS, LANES), lambda i, e: (0, i, 0, 0))
    vec = pl.BlockSpec((1, D_MODEL), lambda i, e: (0, 0))
    return pl.pallas_call(
        _peer_kernel,
        grid=(n // t, ne + 2),
        in_specs=[pl.BlockSpec((t, D_MODEL), lambda i, e: (i, 0)),
                  pl.BlockSpec((PEER_EC, D_MODEL), lambda i, e: (chunk(e, 0), 0)),
                  pl.BlockSpec((D_MODEL, PEER_EC), lambda i, e: (0, chunk(e, 2))),
                  rows, rows, big, big, vec, vec],
        out_specs=[pl.BlockSpec((t, D_MODEL), lambda i, e: (i, 0))] * 2,
        out_shape=[jax.ShapeDtypeStruct((n, D_MODEL), F32), jax.ShapeDtypeStruct((n, D_MODEL), BF16)],
        scratch_shapes=[pltpu.VMEM((D_MODEL, t), BF16), pltpu.VMEM((D_MODEL, t), F32),
                        pltpu.VMEM((PEER_EC, t), F32), pltpu.VMEM((PEER_EC, t), F32),
                        pltpu.VMEM((PEER_EC, t), BF16), pltpu.VMEM((PEER_EC, t), BF16)],
        compiler_params=_cparams(("parallel", "arbitrary")),
        name="peer_dense",
    )(x, u, vt, th, a, s2, bn, g, b)


def _rope_tables(seq, dim):
    inv = ROPE_THETA ** (-jnp.arange(0, dim, 2, dtype=F32) / dim)
    ang = jnp.arange(seq, dtype=F32)[:, None] * inv[None, :]
    return jnp.cos(ang), jnp.sin(ang)


def _retention_tables():
    log_g = jnp.log(1.0 - 2.0 ** (-5.0 - jnp.arange(RET_HEADS, dtype=F32)))
    idx = jnp.arange(RET_CHUNK, dtype=F32)
    diff = idx[:, None] - idx[None, :]
    dmat = jnp.where(diff >= 0, jnp.exp(jnp.maximum(diff, 0.0)[None] * log_g[:, None, None]), 0.0)
    xi = jnp.exp((idx[None] + 1.0) * log_g[:, None])[..., None]
    zeta = jnp.exp((RET_CHUNK - 1.0 - idx[None]) * log_g[:, None])[..., None]
    gch = jnp.exp(RET_CHUNK * log_g)[:, None, None]
    return dmat, xi, zeta, gch


def _block_diag(w):
    g, i, o = w.shape
    eye = jnp.eye(g, dtype=w.dtype)
    return jnp.einsum('gio,gh->giho', w, eye).reshape(g * i, g * o)


def kernel(x, w_in, conv_w, conv_b, lru_wa, lru_ba, lru_wx, lru_bx, lru_lambda, w_branch, w_out,
           ln1_g, ln1_b, peer_wq, peer_k1, peer_k2, peer_u, peer_v, ln2_g, ln2_b):
    batch, seq, d = x.shape
    n = batch * seq
    depth = w_in.shape[0]
    cos_a, sin_a = _rope_tables(seq, ATT_E)
    cos_a2 = jnp.concatenate([cos_a, cos_a], axis=1)
    sin_a2 = jnp.concatenate([-sin_a, sin_a], axis=1)
    cos_r, sin_r = _rope_tables(seq, RET_DK)
    dmat, xi, zeta, gch = _retention_tables()

    xf = x.reshape(n, d)
    xb = xf.astype(BF16)
    for l in range(depth):
        w_att = w_in[l][:, :D_ATT].astype(BF16)
        w_rest = w_in[l][:, D_ATT:].astype(BF16)
        wg = jnp.concatenate([_block_diag(lru_wa[l]), _block_diag(lru_wx[l])], axis=1).astype(BF16)
        bg = jnp.concatenate([lru_ba[l], lru_bx[l]])[None, :]
        wb = w_branch[l].astype(BF16)
        wba, wbb, wbc = wb[:ATT_W], wb[ATT_W:ATT_W + d], wb[ATT_W + d:]
        wo = w_out[l].astype(BF16)
        wq = peer_wq[l].astype(BF16)
        k1 = peer_k1[l].astype(BF16)
        k2 = peer_k2[l].astype(BF16)
        u = peer_u[l].astype(BF16)
        vt = peer_v[l].T.astype(BF16)

        att3 = _matmul(xb, w_att, F32).reshape(batch, seq, D_ATT)
        rest = _matmul(xb, w_rest, BF16)
        rest3 = rest.reshape(batch, seq, D_REST)
        os_, ls_ = [], []
        for g, (_, dil) in enumerate(ATT_GROUPS):
            o, lse = _attention_group(att3, cos_a2, sin_a2, g, dil)
            os_.append(o)
            ls_.append(lse)
        yb = _lru(rest3, conv_w[l], conv_b[l][None, :], wg, bg, lru_lambda[l][None, :], BF16)
        yc = _retention(rest3, cos_r, sin_r, dmat, xi, zeta, gch, BF16)
        x1 = _merge(os_, ls_, yb.reshape(n, d), yc.reshape(n, d), rest, xf, wba, wbb, wbc, wo,
                    ln1_g[l][None, :], ln1_b[l][None, :])
        th, a, s2, bn = _peer_select(x1, wq, k1, k2)
        xf, xb = _peer_dense(x1, u, vt, th, a, s2, bn, ln2_g[l][None, :], ln2_b[l][None, :])
    return xf.reshape(batch, seq, d)
```

```python
import functools
import math

import numpy as np
import jax
import jax.numpy as jnp
from jax import lax
from jax.experimental import pallas as pl
from jax.experimental.pallas import tpu as pltpu

F32 = jnp.float32
BF16 = jnp.bfloat16

D_MODEL = 1024
DEPTH = 4
ROPE_THETA = 10000.0
LN_EPS = 1e-5
ATT_GROUPS = ((128, 1), (512, 4), (2048, 16))
ATT_HPG = 4
ATT_E = 128
ATT_BLK = 128
ATT_W = ATT_HPG * ATT_E
LRU_BLOCKS = 16
CONV_WIDTH = 4
LRU_C = 8.0
RET_HEADS = 4
RET_DK = 256
RET_CHUNK = 128
PEER_HEADS = 8
PEER_NKEYS = 128
PEER_QDIM = 256
PEER_TOPK = 16
ALPHA = (2 * DEPTH) ** 0.25

D_ATT = 3 * 1536
D_REST = 9 * 1024
COL_LX, COL_LG, COL_RQ, COL_RK, COL_RV, COL_RG, COL_GA, COL_GB, COL_GC = (2 * i for i in range(9))
COL_AQ, COL_AK, COL_AV = 0, 3, 6

VMEM_LIMIT = 56 * 1024 * 1024
LANES = 128
NEG_INF = float("-inf")


def _cparams(sem):
    return pltpu.CompilerParams(dimension_semantics=sem, vmem_limit_bytes=VMEM_LIMIT)


def _layer_norm(z, g, b):
    mu = jnp.mean(z, axis=-1, keepdims=True)
    zc = z - mu
    var = jnp.mean(zc * zc, axis=-1, keepdims=True)
    return zc * lax.rsqrt(var + LN_EPS) * g + b


def _gelu(x):
    return 0.5 * x * (1.0 + lax.erf(x * np.float32(math.sqrt(0.5))))


def _mm_kernel(x_ref, w_ref, o_ref):
    o_ref[...] = jnp.dot(x_ref[...], w_ref[...], preferred_element_type=F32).astype(o_ref.dtype)


def _matmul(x, w, out_dtype, tm=512, tn=1536):
    n, k = x.shape
    m = w.shape[1]
    return pl.pallas_call(
        _mm_kernel,
        grid=(m // tn, n // tm),
        in_specs=[pl.BlockSpec((tm, k), lambda j, i: (i, 0)),
                  pl.BlockSpec((k, tn), lambda j, i: (0, j))],
        out_specs=pl.BlockSpec((tm, tn), lambda j, i: (i, j)),
        out_shape=jax.ShapeDtypeStruct((n, m), out_dtype),
        compiler_params=_cparams(("parallel", "parallel")),
        name="in_proj",
    )(x, w)


ATT_SPAN = 2048


def _att_kernel(dil, q_ref, kp_ref, kc_ref, vp_ref, vc_ref, cc_ref, sc_ref, cp_ref, sp_ref,
                o_ref, lse_ref, qs_ref, ks_ref, kps_ref):
    n = pl.program_id(2)
    halo = dil * ATT_BLK

    def rope(t, c, s):
        return t * c + pltpu.roll(t, ATT_E // 2, axis=1) * s

    cos_c, sin_c = cc_ref[...], sc_ref[...]
    qs_ref[...] = rope(q_ref[0].astype(F32), cos_c, sin_c)
    ks_ref[...] = rope(kc_ref[0].astype(F32), cos_c, sin_c)
    kps_ref[...] = rope(kp_ref[0, ATT_SPAN - halo:, :].astype(F32),
                        cp_ref[ATT_SPAN - halo:, :], sp_ref[ATT_SPAN - halo:, :])

    qi = lax.broadcasted_iota(jnp.int32, (ATT_BLK, 2 * ATT_BLK), 0)
    kj = lax.broadcasted_iota(jnp.int32, (ATT_BLK, 2 * ATT_BLK), 1)
    delta = qi + ATT_BLK - kj
    inside = jnp.where(delta >= 0, jnp.where(delta <= ATT_BLK, 1, 0), 0)
    has_prev = jnp.where(n > 0, 1, 0)
    valid_mid = inside > 0
    valid_first = jnp.where(kj >= ATT_BLK, inside, inside * has_prev) > 0

    def rows(start):
        return pl.ds(start, ATT_BLK, stride=dil) if dil > 1 else pl.ds(start, ATT_BLK)

    for r in range(dil):
        for jb in range(ATT_SPAN // halo):
            cur = rows(r + jb * halo)
            q = qs_ref[cur, :].astype(BF16)
            if jb == 0:
                prev = rows(ATT_SPAN - halo + r)
                kp, vp = kps_ref[rows(r), :], vp_ref[0, prev, :]
            else:
                prev = rows(r + (jb - 1) * halo)
                kp, vp = ks_ref[prev, :], vc_ref[0, prev, :]
            kcat = jnp.concatenate([kp, ks_ref[cur, :]], axis=0).astype(BF16)
            vcat = jnp.concatenate([vp, vc_ref[0, cur, :]], axis=0).astype(BF16)
            s = lax.dot_general(q, kcat, (((1,), (1,)), ((), ())), preferred_element_type=F32)
            s = jnp.where(valid_first if jb == 0 else valid_mid, s * np.float32(ATT_E ** -0.5), NEG_INF)
            m = jnp.max(s, axis=-1, keepdims=True)
            p = jnp.exp(s - m)
            l = jnp.sum(p, axis=-1, keepdims=True)
            o = jnp.dot(p.astype(BF16), vcat, preferred_element_type=F32)
            o_ref[0, cur, :] = (o / l).astype(o_ref.dtype)
            lse_ref[0, cur, :] = jnp.broadcast_to(m + jnp.log(l), (ATT_BLK, ATT_E))


def _attention_group(proj3, cos_t, sin_t, g, dil):
    batch, seq, _ = proj3.shape
    assert seq % ATT_SPAN == 0 and ATT_SPAN % (dil * ATT_BLK) == 0
    heads_col = lambda col: col * (512 // ATT_E) + g * ATT_HPG

    def cur(col):
        return pl.BlockSpec((1, ATT_SPAN, ATT_E), lambda b, h, n: (b, n, heads_col(col) + h))

    def prev(col):
        return pl.BlockSpec((1, ATT_SPAN, ATT_E),
                            lambda b, h, n: (b, jnp.maximum(n - 1, 0), heads_col(col) + h))

    tab_c = pl.BlockSpec((ATT_SPAN, ATT_E), lambda b, h, n: (n, 0))
    tab_p = pl.BlockSpec((ATT_SPAN, ATT_E), lambda b, h, n: (jnp.maximum(n - 1, 0), 0))
    out_spec = pl.BlockSpec((1, ATT_SPAN, ATT_E), lambda b, h, n: (b, n, h))
    o, lse = pl.pallas_call(
        functools.partial(_att_kernel, dil),
        grid=(batch, ATT_HPG, seq // ATT_SPAN),
        in_specs=[cur(COL_AQ), prev(COL_AK), cur(COL_AK), prev(COL_AV), cur(COL_AV),
                  tab_c, tab_c, tab_p, tab_p],
        out_specs=[out_spec, out_spec],
        out_shape=[jax.ShapeDtypeStruct((batch, seq, ATT_W), F32)] * 2,
        scratch_shapes=[pltpu.VMEM((ATT_SPAN, ATT_E), F32), pltpu.VMEM((ATT_SPAN, ATT_E), F32),
                        pltpu.VMEM((dil * ATT_BLK, ATT_E), F32)],
        compiler_params=_cparams(("parallel", "parallel", "parallel")),
        name=f"dilated_attention_d{dil}",
    )(proj3, proj3, proj3, proj3, proj3, cos_t, sin_t, cos_t, sin_t)
    return o.reshape(batch * seq, ATT_W), lse.reshape(batch * seq, ATT_W)


LRU_TS = 256
LRU_HALO = 8


def _lru_kernel(lx_ref, lg_ref, cw_ref, cb_ref, wg_ref, bg_ref, lam_ref, y_ref, ext_ref, h_ref):
    s_idx = pl.program_id(1)
    ts, c = lx_ref.shape[1], lx_ref.shape[2]

    @pl.when(s_idx == 0)
    def _():
        ext_ref[0:LRU_HALO, :] = jnp.zeros((LRU_HALO, c), F32)
        h_ref[...] = jnp.zeros_like(h_ref)

    ext_ref[LRU_HALO:, :] = lx_ref[0].astype(F32)
    xc = jnp.broadcast_to(cb_ref[...], (ts, c))
    for w in range(CONV_WIDTH):
        off = LRU_HALO - (CONV_WIDTH - 1) + w
        xc = xc + ext_ref[off:off + ts, :] * cw_ref[w:w + 1, :]
    ext_ref[0:LRU_HALO, :] = ext_ref[ts:ts + LRU_HALO, :]

    gates = jnp.dot(xc.astype(BF16), wg_ref[...], preferred_element_type=F32) + bg_ref[...]
    r = jax.nn.sigmoid(gates[:, :c])
    i = jax.nn.sigmoid(gates[:, c:])
    z = -lam_ref[...]
    softplus = jnp.maximum(z, 0.0) + jnp.log1p(jnp.exp(-jnp.abs(z)))
    log_a = (-LRU_C) * r * softplus
    a = jnp.exp(log_a)
    b = jnp.sqrt(1.0 - a * a) * (i * xc)

    row = lax.broadcasted_iota(jnp.int32, (ts, c), 0)
    k = 1
    while k < ts:
        keep = row >= k
        a_s = jnp.where(keep, pltpu.roll(a, k, axis=0), 1.0)
        b_s = jnp.where(keep, pltpu.roll(b, k, axis=0), 0.0)
        b = a * b_s + b
        a = a * a_s
        k *= 2
    h = b + a * h_ref[...]
    h_ref[...] = h[ts - 1:ts, :]
    y_ref[0] = (h * _gelu(lg_ref[0].astype(F32))).astype(y_ref.dtype)


def _lru(proj3, conv_w, conv_b, wg, bg, lam, out_dtype):
    batch, seq, _ = proj3.shape
    c = D_MODEL
    row = lambda b, s: (0, 0)
    return pl.pallas_call(
        _lru_kernel,
        grid=(batch, seq // LRU_TS),
        in_specs=[pl.BlockSpec((1, LRU_TS, c), lambda b, s: (b, s, COL_LX // 2)),
                  pl.BlockSpec((1, LRU_TS, c), lambda b, s: (b, s, COL_LG // 2)),
                  pl.BlockSpec((CONV_WIDTH, c), row),
                  pl.BlockSpec((1, c), row),
                  pl.BlockSpec((c, 2 * c), row),
                  pl.BlockSpec((1, 2 * c), row),
                  pl.BlockSpec((1, c), row)],
        out_specs=pl.BlockSpec((1, LRU_TS, c), lambda b, s: (b, s, 0)),
        out_shape=jax.ShapeDtypeStruct((batch, seq, c), out_dtype),
        scratch_shapes=[pltpu.VMEM((LRU_TS + LRU_HALO, c), F32), pltpu.VMEM((1, c), F32)],
        compiler_params=_cparams(("parallel", "arbitrary")),
        name="rg_lru",
    )(proj3, proj3, conv_w, conv_b, wg, bg, lam)


def _ret_kernel(q_ref, k_ref, v_ref, g_ref, cos_ref, sin_ref, dmat_ref, xi_ref, zeta_ref, gch_ref,
                y_ref, r_ref):
    @pl.when(pl.program_id(1) == 0)
    def _():
        r_ref[...] = jnp.zeros_like(r_ref)

    cos, sin = cos_ref[...], sin_ref[...]
    half = RET_DK // 2

    def rope(t):
        t1, t2 = t[:, :half], t[:, half:]
        return jnp.concatenate([t1 * cos - t2 * sin, t1 * sin + t2 * cos], axis=1)

    for h in range(RET_HEADS):
        sl = slice(h * RET_DK, (h + 1) * RET_DK)
        q = rope(q_ref[0, :, sl].astype(F32)).astype(BF16)
        k = rope(k_ref[0, :, sl].astype(F32)) * np.float32(RET_DK ** -0.5)
        v = v_ref[0, :, sl].astype(BF16)
        state = r_ref[h]
        qk = lax.dot_general(q, k.astype(BF16), (((1,), (1,)), ((), ())), preferred_element_type=F32)
        inner = jnp.dot((qk * dmat_ref[h]).astype(BF16), v, preferred_element_type=F32)
        cross = jnp.dot(q, state.astype(BF16), preferred_element_type=F32) * xi_ref[h]
        kz = (k * zeta_ref[h]).astype(BF16)
        r_ref[h] = gch_ref[h] * state + lax.dot_general(
            kz, v, (((0,), (0,)), ((), ())), preferred_element_type=F32)
        ret = inner + cross
        mu = jnp.mean(ret, axis=-1, keepdims=True)
        rc = ret - mu
        var = jnp.mean(rc * rc, axis=-1, keepdims=True)
        gate = g_ref[0, :, sl].astype(F32)
        y_ref[0, :, sl] = (gate * jax.nn.sigmoid(gate) * (rc * lax.rsqrt(var + LN_EPS))).astype(y_ref.dtype)


def _retention(proj3, cos_r, sin_r, dmat, xi, zeta, gch, out_dtype):
    batch, seq, _ = proj3.shape
    c = RET_HEADS * RET_DK
    blk = lambda col: pl.BlockSpec((1, RET_CHUNK, c), lambda b, n: (b, n, col // 2))
    tab = pl.BlockSpec((RET_CHUNK, RET_DK // 2), lambda b, n: (n, 0))
    const3 = lambda shape: pl.BlockSpec(shape, lambda b, n: (0, 0, 0))
    return pl.pallas_call(
        _ret_kernel,
        grid=(batch, seq // RET_CHUNK),
        in_specs=[blk(COL_RQ), blk(COL_RK), blk(COL_RV), blk(COL_RG), tab, tab,
                  const3((RET_HEADS, RET_CHUNK, RET_CHUNK)),
                  const3((RET_HEADS, RET_CHUNK, 1)),
                  const3((RET_HEADS, RET_CHUNK, 1)),
                  const3((RET_HEADS, 1, 1))],
        out_specs=pl.BlockSpec((1, RET_CHUNK, c), lambda b, n: (b, n, 0)),
        out_shape=jax.ShapeDtypeStruct((batch, seq, c), out_dtype),
        scratch_shapes=[pltpu.VMEM((RET_HEADS, RET_DK, RET_DK), F32)],
        compiler_params=_cparams(("parallel", "arbitrary")),
        name="retention",
    )(proj3, proj3, proj3, proj3, cos_r, sin_r, dmat, xi, zeta, gch)


MERGE_TM = 256


def _merge_kernel(o0, o1, o2, l0, l1, l2, yb_ref, yc_ref, ga_ref, gb_ref, gc_ref, x_ref,
                  wba_ref, wbb_ref, wbc_ref, wo_ref, g_ref, b_ref, out_ref):
    e0, e1, e2 = l0[...], l1[...], l2[...]
    mx = jnp.maximum(jnp.maximum(e0, e1), e2)
    w0, w1, w2 = jnp.exp(e0 - mx), jnp.exp(e1 - mx), jnp.exp(e2 - mx)
    ya = (w0 * o0[...] + w1 * o1[...] + w2 * o2[...]) / (w0 + w1 + w2)
    za = jnp.dot(ya.astype(BF16), wba_ref[...], preferred_element_type=F32)
    zb = jnp.dot(yb_ref[...], wbb_ref[...], preferred_element_type=F32)
    zc = jnp.dot(yc_ref[...], wbc_ref[...], preferred_element_type=F32)
    merged = (jax.nn.sigmoid(ga_ref[...].astype(F32)) * za
              + jax.nn.sigmoid(gb_ref[...].astype(F32)) * zb
              + jax.nn.sigmoid(gc_ref[...].astype(F32)) * zc)
    mix = jnp.dot(merged.astype(BF16), wo_ref[...], preferred_element_type=F32)
    out_ref[...] = _layer_norm(np.float32(ALPHA) * x_ref[...] + mix, g_ref[...], b_ref[...])


def _merge(os_, ls_, yb, yc, proj, x, wba, wbb, wbc, wo, g, b):
    n = x.shape[0]
    c = D_MODEL
    tm = MERGE_TM
    row512 = pl.BlockSpec((tm, ATT_W), lambda i: (i, 0))
    row = lambda col: pl.BlockSpec((tm, c), lambda i: (i, col))
    full = lambda shape: pl.BlockSpec(shape, lambda i: (0, 0))
    return pl.pallas_call(
        _merge_kernel,
        grid=(n // tm,),
        in_specs=[row512] * 6 + [row(0), row(0), row(COL_GA // 2), row(COL_GB // 2), row(COL_GC // 2),
                                 row(0), full((ATT_W, c)), full((c, c)), full((c, c)), full((c, c)),
                                 full((1, c)), full((1, c))],
        out_specs=row(0),
        out_shape=jax.ShapeDtypeStruct((n, c), F32),
        compiler_params=_cparams(("parallel",)),
        name="merge_out_ln",
    )(*os_, *ls_, yb, yc, proj, proj, proj, x, wba, wbb, wbc, wo, g, b)


PEER_TQ = 256


PEER_NO_RANK = 127.0


def _top16(s, want_rank=False):
    work = s
    vals = []
    rank = jnp.full(s.shape, PEER_NO_RANK, F32) if want_rank else None
    for r in range(PEER_TOPK):
        m = jnp.max(work, axis=0, keepdims=True)
        vals.append(m)
        top = work == m
        if want_rank:
            rank = jnp.where(top, np.float32(r), rank)
        work = jnp.where(top, NEG_INF, work)
    return vals, rank


def _pack_pairs(v):
    return pltpu.bitcast(v.astype(BF16), jnp.uint32)


def _peer_sel_kernel(x_ref, wq_ref, k1_ref, k2_ref, cnt_ref, a_ref, rk_ref, bn_ref):
    q = jnp.dot(x_ref[...].astype(BF16), wq_ref[...], preferred_element_type=F32).astype(BF16)
    half = PEER_QDIM // 2
    nt = (((1,), (1,)), ((), ()))
    for h in range(PEER_HEADS):
        q1 = q[:, h * PEER_QDIM:h * PEER_QDIM + half]
        q2 = q[:, h * PEER_QDIM + half:(h + 1) * PEER_QDIM]
        s1 = lax.dot_general(k1_ref[h], q1, nt, preferred_element_type=F32)
        s2 = lax.dot_general(k2_ref[h], q2, nt, preferred_element_type=F32)
        v1, _ = _top16(s1)
        v2, rank2 = _top16(s2, want_rank=True)
        v1s = jnp.concatenate(v1, axis=0)
        v2s = jnp.concatenate(v2, axis=0)
        cands = [v1s + v2[0]] + [v1s[0:8] + v2[b] for b in range(1, 8)] + [v1[0] + v2s[8:16]]
        cand = jnp.concatenate(cands, axis=0)
        work = cand
        for it in range(PEER_TOPK):
            tau = jnp.max(work, axis=0, keepdims=True)
            if it + 1 < PEER_TOPK:
                work = jnp.where(work == tau, NEG_INF, work)
        cmax = v1[0] + v2[0]
        z = jnp.sum(jnp.where(cand >= tau, jnp.exp(cand - cmax), 0.0), axis=0, keepdims=True)
        cnt = jnp.zeros(s1.shape, F32)
        for a in range(PEER_TOPK):
            c_a = jnp.sum(jnp.where(v1[a] + v2s >= tau, 1.0, 0.0), axis=0, keepdims=True)
            cnt = jnp.where(s1 == v1[a], c_a, cnt)
        cnt_ref[h] = cnt
        a_ref[h] = jnp.exp(s1 - v1[0])
        bn = jnp.exp(s2 - v2[0]) * (0.5 / z)
        for c in range(s2.shape[1] // LANES):
            rk_ref[h, c] = _pack_pairs(rank2[:, c * LANES:(c + 1) * LANES])
            bn_ref[h, c] = _pack_pairs(bn[:, c * LANES:(c + 1) * LANES])


def _peer_select(x, wq, k1, k2):
    n = x.shape[0]
    tq = PEER_TQ
    hq = PEER_HEADS * PEER_QDIM
    big = pl.BlockSpec((PEER_HEADS, PEER_NKEYS, tq), lambda t: (0, 0, t))
    tiled = pl.BlockSpec((PEER_HEADS, tq // LANES, PEER_NKEYS // 2, LANES), lambda t: (0, t, 0, 0))
    keys = pl.BlockSpec((PEER_HEADS, PEER_NKEYS, PEER_QDIM // 2), lambda t: (0, 0, 0))
    shp = jax.ShapeDtypeStruct((PEER_HEADS, PEER_NKEYS, n), F32)
    shp_tiled = jax.ShapeDtypeStruct((PEER_HEADS, n // LANES, PEER_NKEYS // 2, LANES), jnp.uint32)
    return pl.pallas_call(
        _peer_sel_kernel,
        grid=(n // tq,),
        in_specs=[pl.BlockSpec((tq, D_MODEL), lambda t: (t, 0)),
                  pl.BlockSpec((D_MODEL, hq), lambda t: (0, 0)), keys, keys],
        out_specs=[big, big, tiled, tiled],
        out_shape=[shp, shp, shp_tiled, shp_tiled],
        compiler_params=_cparams(("parallel",)),
        name="peer_select",
    )(x, wq, k1, k2)


PEER_T = 512
PEER_EC = 1024
PEER_IPC = PEER_EC // PEER_NKEYS


def _peer_kernel(x_ref, u_ref, vt_ref, cnt_ref, a_ref, rk_ref, bn_ref, g_ref, b_ref,
                 o_ref, ob_ref, xt_ref, acc_ref, pre0_ref, pre1_ref, act0_ref, act1_ref):
    e = pl.program_id(1)
    ne = pl.num_programs(1) - 2
    t = x_ref.shape[0]

    quads = [(slice(mh * 512, (mh + 1) * 512), slice(nh * 256, (nh + 1) * 256))
             for mh in range(2) for nh in range(t // 256)]

    def first_matmul(pre_ref):
        def piece(rows, cols):
            pre_ref[rows, cols] = jnp.dot(u_ref[rows, :], xt_ref[:, cols], preferred_element_type=F32)
        return [functools.partial(piece, r, c) for r, c in quads]

    def second_matmul(act_ref):
        def piece(rows, cols):
            acc_ref[rows, cols] += jnp.dot(vt_ref[rows, :], act_ref[:, cols], preferred_element_type=F32)
        return [functools.partial(piece, r, c) for r, c in quads]

    def gate(pre_ref, act_ref):
        def piece(ii, lt):
            rows = slice(ii * PEER_NKEYS, (ii + 1) * PEER_NKEYS)
            lanes = slice(lt * LANES, (lt + 1) * LANES)
            tile = (PEER_NKEYS, LANES)
            w = None
            for h in range(PEER_HEADS):
                cnt = jnp.broadcast_to(cnt_ref[h, ii:ii + 1, lanes].astype(BF16), tile)
                a = jnp.broadcast_to(a_ref[h, ii:ii + 1, lanes].astype(BF16), tile)
                hit = pltpu.bitcast(rk_ref[h, lt], BF16) < cnt
                term = jnp.where(hit, pltpu.bitcast(bn_ref[h, lt], BF16), jnp.zeros((), BF16)) * a
                w = term if w is None else w + term
            p = pre_ref[rows, lanes]
            gelu2 = p * (1.0 + lax.erf(p * np.float32(math.sqrt(0.5))))
            act_ref[rows, lanes] = gelu2.astype(BF16) * w
        return [functools.partial(piece, ii, lt) for ii in range(PEER_IPC) for lt in range(t // LANES)]

    def interleave(mxu_pieces, vpu_pieces):
        per = -(-len(vpu_pieces) // max(len(mxu_pieces), 1)) if vpu_pieces else 0
        for k, mp in enumerate(mxu_pieces):
            mp()
            for vp in vpu_pieces[k * per:(k + 1) * per]:
                vp()
        for vp in vpu_pieces[len(mxu_pieces) * per:]:
            vp()

    @pl.when(e == 0)
    def _():
        xt_ref[...] = x_ref[...].T.astype(BF16)
        acc_ref[...] = jnp.zeros_like(acc_ref)
        interleave(first_matmul(pre0_ref), [])

    @pl.when(e == 1)
    def _():
        interleave(first_matmul(pre1_ref), gate(pre0_ref, act0_ref))

    @pl.when(jnp.logical_and(jnp.logical_and(e >= 2, e < ne), e % 2 == 0))
    def _():
        interleave(first_matmul(pre0_ref) + second_matmul(act0_ref), gate(pre1_ref, act1_ref))

    @pl.when(jnp.logical_and(jnp.logical_and(e >= 2, e < ne), e % 2 == 1))
    def _():
        interleave(first_matmul(pre1_ref) + second_matmul(act1_ref), gate(pre0_ref, act0_ref))

    @pl.when(e == ne)
    def _():
        interleave(second_matmul(act0_ref), gate(pre1_ref, act1_ref))

    @pl.when(e == ne + 1)
    def _():
        interleave(second_matmul(act1_ref), [])
        y = acc_ref[...].T
        out = _layer_norm(np.float32(ALPHA) * x_ref[...] + y, g_ref[...], b_ref[...])
        o_ref[...] = out
        ob_ref[...] = out.astype(BF16)


def _peer_dense(x, u, vt, cnt, a, rk, bn, g, b):
    n = x.shape[0]
    t = PEER_T
    ne = u.shape[0] // PEER_EC
    assert ne % 2 == 0 and ne >= 2
    chunk = lambda e, lag: jnp.clip(e - lag, 0, ne - 1)
    rows = pl.BlockSpec((PEER_HEADS, PEER_IPC, t), lambda i, e: (0, chunk(e, 1), i))
    big = pl.BlockSpec((PEER_HEADS, t // LANES, PEER_NKEYS // 2, LANES), lambda i, e: (0, i, 0, 0))
    vec = pl.BlockSpec((1, D_MODEL), lambda i, e: (0, 0))
    return pl.pallas_call(
        _peer_kernel,
        grid=(n // t, ne + 2),
        in_specs=[pl.BlockSpec((t, D_MODEL), lambda i, e: (i, 0)),
                  pl.BlockSpec((PEER_EC, D_MODEL), lambda i, e: (chunk(e, 0), 0)),
                  pl.BlockSpec((D_MODEL, PEER_EC), lambda i, e: (0, chunk(e, 2))),
                  rows, rows, big, big, vec, vec],
        out_specs=[pl.BlockSpec((t, D_MODEL), lambda i, e: (i, 0))] * 2,
        out_shape=[jax.ShapeDtypeStruct((n, D_MODEL), F32), jax.ShapeDtypeStruct((n, D_MODEL), BF16)],
        scratch_shapes=[pltpu.VMEM((D_MODEL, t), BF16), pltpu.VMEM((D_MODEL, t), F32),
                        pltpu.VMEM((PEER_EC, t), F32), pltpu.VMEM((PEER_EC, t), F32),
                        pltpu.VMEM((PEER_EC, t), BF16), pltpu.VMEM((PEER_EC, t), BF16)],
        compiler_params=_cparams(("parallel", "arbitrary")),
        name="peer_dense",
    )(x, u, vt, cnt, a, rk, bn, g, b)


def _rope_tables(seq, dim):
    inv = ROPE_THETA ** (-jnp.arange(0, dim, 2, dtype=F32) / dim)
    ang = jnp.arange(seq, dtype=F32)[:, None] * inv[None, :]
    return jnp.cos(ang), jnp.sin(ang)


def _retention_tables():
    log_g = jnp.log(1.0 - 2.0 ** (-5.0 - jnp.arange(RET_HEADS, dtype=F32)))
    idx = jnp.arange(RET_CHUNK, dtype=F32)
    diff = idx[:, None] - idx[None, :]
    dmat = jnp.where(diff >= 0, jnp.exp(jnp.maximum(diff, 0.0)[None] * log_g[:, None, None]), 0.0)
    xi = jnp.exp((idx[None] + 1.0) * log_g[:, None])[..., None]
    zeta = jnp.exp((RET_CHUNK - 1.0 - idx[None]) * log_g[:, None])[..., None]
    gch = jnp.exp(RET_CHUNK * log_g)[:, None, None]
    return dmat, xi, zeta, gch


def _block_diag(w):
    g, i, o = w.shape
    eye = jnp.eye(g, dtype=w.dtype)
    return jnp.einsum('gio,gh->giho', w, eye).reshape(g * i, g * o)


def kernel(x, w_in, conv_w, conv_b, lru_wa, lru_ba, lru_wx, lru_bx, lru_lambda, w_branch, w_out,
           ln1_g, ln1_b, peer_wq, peer_k1, peer_k2, peer_u, peer_v, ln2_g, ln2_b):
    batch, seq, d = x.shape
    n = batch * seq
    depth = w_in.shape[0]
    cos_a, sin_a = _rope_tables(seq, ATT_E)
    cos_a2 = jnp.concatenate([cos_a, cos_a], axis=1)
    sin_a2 = jnp.concatenate([-sin_a, sin_a], axis=1)
    cos_r, sin_r = _rope_tables(seq, RET_DK)
    dmat, xi, zeta, gch = _retention_tables()

    xf = x.reshape(n, d)
    xb = xf.astype(BF16)
    for l in range(depth):
        w_att = w_in[l][:, :D_ATT].astype(BF16)
        w_rest = w_in[l][:, D_ATT:].astype(BF16)
        wg = jnp.concatenate([_block_diag(lru_wa[l]), _block_diag(lru_wx[l])], axis=1).astype(BF16)
        bg = jnp.concatenate([lru_ba[l], lru_bx[l]])[None, :]
        wb = w_branch[l].astype(BF16)
        wba, wbb, wbc = wb[:ATT_W], wb[ATT_W:ATT_W + d], wb[ATT_W + d:]
        wo = w_out[l].astype(BF16)
        wq = peer_wq[l].astype(BF16)
        k1 = peer_k1[l].astype(BF16)
        k2 = peer_k2[l].astype(BF16)
        u = peer_u[l].astype(BF16)
        vt = peer_v[l].T.astype(BF16)

        att3 = _matmul(xb, w_att, F32).reshape(batch, seq, D_ATT)
        rest = _matmul(xb, w_rest, BF16)
        rest3 = rest.reshape(batch, seq, D_REST)
        os_, ls_ = [], []
        for g, (_, dil) in enumerate(ATT_GROUPS):
            o, lse = _attention_group(att3, cos_a2, sin_a2, g, dil)
            os_.append(o)
            ls_.append(lse)
        yb = _lru(rest3, conv_w[l], conv_b[l][None, :], wg, bg, lru_lambda[l][None, :], BF16)
        yc = _retention(rest3, cos_r, sin_r, dmat, xi, zeta, gch, BF16)
        x1 = _merge(os_, ls_, yb.reshape(n, d), yc.reshape(n, d), rest, xf, wba, wbb, wbc, wo,
                    ln1_g[l][None, :], ln1_b[l][None, :])
        cnt, a, rk, bn = _peer_select(x1, wq, k1, k2)
        xf, xb = _peer_dense(x1, u, vt, cnt, a, rk, bn, ln2_g[l][None, :], ln2_b[l][None, :])
    return xf.reshape(batch, seq, d)
```

```python
import functools
import math

import numpy as np
import jax
import jax.numpy as jnp
from jax import lax
from jax.experimental import pallas as pl
from jax.experimental.pallas import tpu as pltpu

F32 = jnp.float32
BF16 = jnp.bfloat16

D_MODEL = 1024
DEPTH = 4
ROPE_THETA = 10000.0
LN_EPS = 1e-5
ATT_GROUPS = ((128, 1), (512, 4), (2048, 16))
ATT_HPG = 4
ATT_E = 128
ATT_BLK = 128
ATT_W = ATT_HPG * ATT_E
LRU_BLOCKS = 16
CONV_WIDTH = 4
LRU_C = 8.0
RET_HEADS = 4
RET_DK = 256
RET_CHUNK = 128
PEER_HEADS = 8
PEER_NKEYS = 128
PEER_QDIM = 256
PEER_TOPK = 16
ALPHA = (2 * DEPTH) ** 0.25

D_ATT = 3 * 1536
D_REST = 9 * 1024
COL_LX, COL_LG, COL_RQ, COL_RK, COL_RV, COL_RG, COL_GA, COL_GB, COL_GC = (2 * i for i in range(9))
COL_AQ, COL_AK, COL_AV = 0, 3, 6

VMEM_LIMIT = 56 * 1024 * 1024
LANES = 128
NEG_INF = float("-inf")


def _cparams(sem):
    return pltpu.CompilerParams(dimension_semantics=sem, vmem_limit_bytes=VMEM_LIMIT)


def _layer_norm(z, g, b):
    mu = jnp.mean(z, axis=-1, keepdims=True)
    zc = z - mu
    var = jnp.mean(zc * zc, axis=-1, keepdims=True)
    return zc * lax.rsqrt(var + LN_EPS) * g + b


def _gelu(x):
    return 0.5 * x * (1.0 + lax.erf(x * np.float32(math.sqrt(0.5))))


def _mm_kernel(x_ref, w_ref, o_ref):
    o_ref[...] = jnp.dot(x_ref[...], w_ref[...], preferred_element_type=F32).astype(o_ref.dtype)


def _matmul(x, w, out_dtype, tm=512, tn=1536):
    n, k = x.shape
    m = w.shape[1]
    return pl.pallas_call(
        _mm_kernel,
        grid=(m // tn, n // tm),
        in_specs=[pl.BlockSpec((tm, k), lambda j, i: (i, 0)),
                  pl.BlockSpec((k, tn), lambda j, i: (0, j))],
        out_specs=pl.BlockSpec((tm, tn), lambda j, i: (i, j)),
        out_shape=jax.ShapeDtypeStruct((n, m), out_dtype),
        compiler_params=_cparams(("parallel", "parallel")),
        name="in_proj",
    )(x, w)


def _mm_heads_kernel(x_ref, w_ref, o_ref):
    res = jnp.dot(x_ref[...], w_ref[...], preferred_element_type=F32)
    for h in range(o_ref.shape[1]):
        o_ref[0, h] = res[:, h * ATT_E:(h + 1) * ATT_E]


def _matmul_heads(x, w, batch, seq, tm=512, tn=1536):
    k = x.shape[1]
    heads = w.shape[1] // ATT_E
    hpt = tn // ATT_E
    spb = seq // tm
    return pl.pallas_call(
        _mm_heads_kernel,
        grid=(heads // hpt, batch * spb),
        in_specs=[pl.BlockSpec((tm, k), lambda j, i: (i, 0)),
                  pl.BlockSpec((k, tn), lambda j, i: (0, j))],
        out_specs=pl.BlockSpec((1, hpt, tm, ATT_E), lambda j, i: (i // spb, j, i % spb, 0)),
        out_shape=jax.ShapeDtypeStruct((batch, heads, seq, ATT_E), F32),
        compiler_params=_cparams(("parallel", "parallel")),
        name="in_proj_heads",
    )(x, w)


ATT_SPAN = 2048


def _att_kernel(dil, q_ref, kp_ref, kc_ref, vp_ref, vc_ref, cc_ref, sc_ref, cp_ref, sp_ref,
                o_ref, lse_ref, qs_ref, ks_ref, kps_ref):
    n = pl.program_id(2)
    halo = dil * ATT_BLK

    def rope(t, c, s):
        return t * c + pltpu.roll(t, ATT_E // 2, axis=1) * s

    cos_c, sin_c = cc_ref[...], sc_ref[...]
    qs_ref[...] = rope(q_ref[0, 0], cos_c, sin_c)
    ks_ref[...] = rope(kc_ref[0, 0], cos_c, sin_c)
    kps_ref[...] = rope(kp_ref[0, 0, ATT_SPAN - halo:, :],
                        cp_ref[ATT_SPAN - halo:, :], sp_ref[ATT_SPAN - halo:, :])

    qi = lax.broadcasted_iota(jnp.int32, (ATT_BLK, 2 * ATT_BLK), 0)
    kj = lax.broadcasted_iota(jnp.int32, (ATT_BLK, 2 * ATT_BLK), 1)
    delta = qi + ATT_BLK - kj
    inside = jnp.where(delta >= 0, jnp.where(delta <= ATT_BLK, 1, 0), 0)
    has_prev = jnp.where(n > 0, 1, 0)
    valid_mid = inside > 0
    valid_first = jnp.where(kj >= ATT_BLK, inside, inside * has_prev) > 0

    def rows(start):
        return pl.ds(start, ATT_BLK, stride=dil) if dil > 1 else pl.ds(start, ATT_BLK)

    for r in range(dil):
        for jb in range(ATT_SPAN // halo):
            cur = rows(r + jb * halo)
            q = qs_ref[cur, :].astype(BF16)
            if jb == 0:
                prev = rows(ATT_SPAN - halo + r)
                kp, vp = kps_ref[rows(r), :], vp_ref[0, 0, prev, :]
            else:
                prev = rows(r + (jb - 1) * halo)
                kp, vp = ks_ref[prev, :], vc_ref[0, 0, prev, :]
            kcat = jnp.concatenate([kp, ks_ref[cur, :]], axis=0).astype(BF16)
            vcat = jnp.concatenate([vp, vc_ref[0, 0, cur, :]], axis=0).astype(BF16)
            s = lax.dot_general(q, kcat, (((1,), (1,)), ((), ())), preferred_element_type=F32)
            s = jnp.where(valid_first if jb == 0 else valid_mid, s * np.float32(ATT_E ** -0.5), NEG_INF)
            m = jnp.max(s, axis=-1, keepdims=True)
            p = jnp.exp(s - m)
            l = jnp.sum(p, axis=-1, keepdims=True)
            o = jnp.dot(p.astype(BF16), vcat, preferred_element_type=F32)
            o_ref[0, cur, :] = (o / l).astype(o_ref.dtype)
            lse_ref[0, cur, :] = jnp.broadcast_to(m + jnp.log(l), (ATT_BLK, ATT_E))


def _attention_group(proj3, cos_t, sin_t, g, dil):
    batch, _, seq, _ = proj3.shape
    assert seq % ATT_SPAN == 0 and ATT_SPAN % (dil * ATT_BLK) == 0
    head_slot = lambda col: col * (512 // ATT_E) + g * ATT_HPG

    def cur(col):
        return pl.BlockSpec((1, 1, ATT_SPAN, ATT_E), lambda b, h, n: (b, head_slot(col) + h, n, 0))

    def prev(col):
        return pl.BlockSpec((1, 1, ATT_SPAN, ATT_E),
                            lambda b, h, n: (b, head_slot(col) + h, jnp.maximum(n - 1, 0), 0))

    tab_c = pl.BlockSpec((ATT_SPAN, ATT_E), lambda b, h, n: (n, 0))
    tab_p = pl.BlockSpec((ATT_SPAN, ATT_E), lambda b, h, n: (jnp.maximum(n - 1, 0), 0))
    out_spec = pl.BlockSpec((1, ATT_SPAN, ATT_E), lambda b, h, n: (b, n, h))
    o, lse = pl.pallas_call(
        functools.partial(_att_kernel, dil),
        grid=(batch, ATT_HPG, seq // ATT_SPAN),
        in_specs=[cur(COL_AQ), prev(COL_AK), cur(COL_AK), prev(COL_AV), cur(COL_AV),
                  tab_c, tab_c, tab_p, tab_p],
        out_specs=[out_spec, out_spec],
        out_shape=[jax.ShapeDtypeStruct((batch, seq, ATT_W), F32)] * 2,
        scratch_shapes=[pltpu.VMEM((ATT_SPAN, ATT_E), F32), pltpu.VMEM((ATT_SPAN, ATT_E), F32),
                        pltpu.VMEM((dil * ATT_BLK, ATT_E), F32)],
        compiler_params=_cparams(("parallel", "parallel", "parallel")),
        name=f"dilated_attention_d{dil}",
    )(proj3, proj3, proj3, proj3, proj3, cos_t, sin_t, cos_t, sin_t)
    return o.reshape(batch * seq, ATT_W), lse.reshape(batch * seq, ATT_W)


LRU_TS = 256
LRU_HALO = 8


def _lru_kernel(lx_ref, lg_ref, cw_ref, cb_ref, wg_ref, bg_ref, lam_ref, y_ref, ext_ref, h_ref):
    s_idx = pl.program_id(1)
    ts, c = lx_ref.shape[1], lx_ref.shape[2]

    @pl.when(s_idx == 0)
    def _():
        ext_ref[0:LRU_HALO, :] = jnp.zeros((LRU_HALO, c), F32)
        h_ref[...] = jnp.zeros_like(h_ref)

    ext_ref[LRU_HALO:, :] = lx_ref[0].astype(F32)
    xc = jnp.broadcast_to(cb_ref[...], (ts, c))
    for w in range(CONV_WIDTH):
        off = LRU_HALO - (CONV_WIDTH - 1) + w
        xc = xc + ext_ref[off:off + ts, :] * cw_ref[w:w + 1, :]
    ext_ref[0:LRU_HALO, :] = ext_ref[ts:ts + LRU_HALO, :]

    gates = jnp.dot(xc.astype(BF16), wg_ref[...], preferred_element_type=F32) + bg_ref[...]
    r = jax.nn.sigmoid(gates[:, :c])
    i = jax.nn.sigmoid(gates[:, c:])
    z = -lam_ref[...]
    softplus = jnp.maximum(z, 0.0) + jnp.log1p(jnp.exp(-jnp.abs(z)))
    log_a = (-LRU_C) * r * softplus
    a = jnp.exp(log_a)
    b = jnp.sqrt(1.0 - a * a) * (i * xc)

    row = lax.broadcasted_iota(jnp.int32, (ts, c), 0)
    k = 1
    while k < ts:
        keep = row >= k
        a_s = jnp.where(keep, pltpu.roll(a, k, axis=0), 1.0)
        b_s = jnp.where(keep, pltpu.roll(b, k, axis=0), 0.0)
        b = a * b_s + b
        a = a * a_s
        k *= 2
    h = b + a * h_ref[...]
    h_ref[...] = h[ts - 1:ts, :]
    y_ref[0] = (h * _gelu(lg_ref[0].astype(F32))).astype(y_ref.dtype)


def _lru(proj3, conv_w, conv_b, wg, bg, lam, out_dtype):
    batch, seq, _ = proj3.shape
    c = D_MODEL
    row = lambda b, s: (0, 0)
    return pl.pallas_call(
        _lru_kernel,
        grid=(batch, seq // LRU_TS),
        in_specs=[pl.BlockSpec((1, LRU_TS, c), lambda b, s: (b, s, COL_LX // 2)),
                  pl.BlockSpec((1, LRU_TS, c), lambda b, s: (b, s, COL_LG // 2)),
                  pl.BlockSpec((CONV_WIDTH, c), row),
                  pl.BlockSpec((1, c), row),
                  pl.BlockSpec((c, 2 * c), row),
                  pl.BlockSpec((1, 2 * c), row),
                  pl.BlockSpec((1, c), row)],
        out_specs=pl.BlockSpec((1, LRU_TS, c), lambda b, s: (b, s, 0)),
        out_shape=jax.ShapeDtypeStruct((batch, seq, c), out_dtype),
        scratch_shapes=[pltpu.VMEM((LRU_TS + LRU_HALO, c), F32), pltpu.VMEM((1, c), F32)],
        compiler_params=_cparams(("parallel", "arbitrary")),
        name="rg_lru",
    )(proj3, proj3, conv_w, conv_b, wg, bg, lam)


def _ret_kernel(q_ref, k_ref, v_ref, g_ref, cos_ref, sin_ref, dmat_ref, xi_ref, zeta_ref, gch_ref,
                y_ref, r_ref):
    @pl.when(pl.program_id(1) == 0)
    def _():
        r_ref[...] = jnp.zeros_like(r_ref)

    cos, sin = cos_ref[...], sin_ref[...]
    half = RET_DK // 2

    def rope(t):
        t1, t2 = t[:, :half], t[:, half:]
        return jnp.concatenate([t1 * cos - t2 * sin, t1 * sin + t2 * cos], axis=1)

    for h in range(RET_HEADS):
        sl = slice(h * RET_DK, (h + 1) * RET_DK)
        q = rope(q_ref[0, :, sl].astype(F32)).astype(BF16)
        k = rope(k_ref[0, :, sl].astype(F32)) * np.float32(RET_DK ** -0.5)
        v = v_ref[0, :, sl].astype(BF16)
        state = r_ref[h]
        qk = lax.dot_general(q, k.astype(BF16), (((1,), (1,)), ((), ())), preferred_element_type=F32)
        inner = jnp.dot((qk * dmat_ref[h]).astype(BF16), v, preferred_element_type=F32)
        cross = jnp.dot(q, state.astype(BF16), preferred_element_type=F32) * xi_ref[h]
        kz = (k * zeta_ref[h]).astype(BF16)
        r_ref[h] = gch_ref[h] * state + lax.dot_general(
            kz, v, (((0,), (0,)), ((), ())), preferred_element_type=F32)
        ret = inner + cross
        mu = jnp.mean(ret, axis=-1, keepdims=True)
        rc = ret - mu
        var = jnp.mean(rc * rc, axis=-1, keepdims=True)
        gate = g_ref[0, :, sl].astype(F32)
        y_ref[0, :, sl] = (gate * jax.nn.sigmoid(gate) * (rc * lax.rsqrt(var + LN_EPS))).astype(y_ref.dtype)


def _retention(proj3, cos_r, sin_r, dmat, xi, zeta, gch, out_dtype):
    batch, seq, _ = proj3.shape
    c = RET_HEADS * RET_DK
    blk = lambda col: pl.BlockSpec((1, RET_CHUNK, c), lambda b, n: (b, n, col // 2))
    tab = pl.BlockSpec((RET_CHUNK, RET_DK // 2), lambda b, n: (n, 0))
    const3 = lambda shape: pl.BlockSpec(shape, lambda b, n: (0, 0, 0))
    return pl.pallas_call(
        _ret_kernel,
        grid=(batch, seq // RET_CHUNK),
        in_specs=[blk(COL_RQ), blk(COL_RK), blk(COL_RV), blk(COL_RG), tab, tab,
                  const3((RET_HEADS, RET_CHUNK, RET_CHUNK)),
                  const3((RET_HEADS, RET_CHUNK, 1)),
                  const3((RET_HEADS, RET_CHUNK, 1)),
                  const3((RET_HEADS, 1, 1))],
        out_specs=pl.BlockSpec((1, RET_CHUNK, c), lambda b, n: (b, n, 0)),
        out_shape=jax.ShapeDtypeStruct((batch, seq, c), out_dtype),
        scratch_shapes=[pltpu.VMEM((RET_HEADS, RET_DK, RET_DK), F32)],
        compiler_params=_cparams(("parallel", "arbitrary")),
        name="retention",
    )(proj3, proj3, proj3, proj3, cos_r, sin_r, dmat, xi, zeta, gch)


MERGE_TM = 256


def _merge_kernel(o0, o1, o2, l0, l1, l2, yb_ref, yc_ref, ga_ref, gb_ref, gc_ref, x_ref,
                  wba_ref, wbb_ref, wbc_ref, wo_ref, g_ref, b_ref, out_ref):
    e0, e1, e2 = l0[...], l1[...], l2[...]
    mx = jnp.maximum(jnp.maximum(e0, e1), e2)
    w0, w1, w2 = jnp.exp(e0 - mx), jnp.exp(e1 - mx), jnp.exp(e2 - mx)
    ya = (w0 * o0[...] + w1 * o1[...] + w2 * o2[...]) / (w0 + w1 + w2)
    za = jnp.dot(ya.astype(BF16), wba_ref[...], preferred_element_type=F32)
    zb = jnp.dot(yb_ref[...], wbb_ref[...], preferred_element_type=F32)
    zc = jnp.dot(yc_ref[...], wbc_ref[...], preferred_element_type=F32)
    merged = (jax.nn.sigmoid(ga_ref[...].astype(F32)) * za
              + jax.nn.sigmoid(gb_ref[...].astype(F32)) * zb
              + jax.nn.sigmoid(gc_ref[...].astype(F32)) * zc)
    mix = jnp.dot(merged.astype(BF16), wo_ref[...], preferred_element_type=F32)
    out_ref[...] = _layer_norm(np.float32(ALPHA) * x_ref[...] + mix, g_ref[...], b_ref[...])


def _merge(os_, ls_, yb, yc, proj, x, wba, wbb, wbc, wo, g, b):
    n = x.shape[0]
    c = D_MODEL
    tm = MERGE_TM
    row512 = pl.BlockSpec((tm, ATT_W), lambda i: (i, 0))
    row = lambda col: pl.BlockSpec((tm, c), lambda i: (i, col))
    full = lambda shape: pl.BlockSpec(shape, lambda i: (0, 0))
    return pl.pallas_call(
        _merge_kernel,
        grid=(n // tm,),
        in_specs=[row512] * 6 + [row(0), row(0), row(COL_GA // 2), row(COL_GB // 2), row(COL_GC // 2),
                                 row(0), full((ATT_W, c)), full((c, c)), full((c, c)), full((c, c)),
                                 full((1, c)), full((1, c))],
        out_specs=row(0),
        out_shape=jax.ShapeDtypeStruct((n, c), F32),
        compiler_params=_cparams(("parallel",)),
        name="merge_out_ln",
    )(*os_, *ls_, yb, yc, proj, proj, proj, x, wba, wbb, wbc, wo, g, b)


PEER_TQ = 256


PEER_NO_RANK = 127.0


def _top16(s, want_rank=False):
    work = s
    vals = []
    rank = jnp.full(s.shape, PEER_NO_RANK, F32) if want_rank else None
    for r in range(PEER_TOPK):
        m = jnp.max(work, axis=0, keepdims=True)
        vals.append(m)
        top = work == m
        if want_rank:
            rank = jnp.where(top, np.float32(r), rank)
        work = jnp.where(top, NEG_INF, work)
    return vals, rank


def _pack_pairs(v):
    return pltpu.bitcast(v.astype(BF16), jnp.uint32)


def _peer_sel_kernel(x_ref, wq_ref, k1_ref, k2_ref, cnt_ref, a_ref, rk_ref, bn_ref):
    q = jnp.dot(x_ref[...].astype(BF16), wq_ref[...], preferred_element_type=F32).astype(BF16)
    half = PEER_QDIM // 2
    nt = (((1,), (1,)), ((), ()))
    for h in range(PEER_HEADS):
        q1 = q[:, h * PEER_QDIM:h * PEER_QDIM + half]
        q2 = q[:, h * PEER_QDIM + half:(h + 1) * PEER_QDIM]
        s1 = lax.dot_general(k1_ref[h], q1, nt, preferred_element_type=F32)
        s2 = lax.dot_general(k2_ref[h], q2, nt, preferred_element_type=F32)
        v1, _ = _top16(s1)
        v2, rank2 = _top16(s2, want_rank=True)
        v1s = jnp.concatenate(v1, axis=0)
        v2s = jnp.concatenate(v2, axis=0)
        cands = [v1s + v2[0]] + [v1s[0:8] + v2[b] for b in range(1, 8)] + [v1[0] + v2s[8:16]]
        cand = jnp.concatenate(cands, axis=0)
        work = cand
        for it in range(PEER_TOPK):
            tau = jnp.max(work, axis=0, keepdims=True)
            if it + 1 < PEER_TOPK:
                work = jnp.where(work == tau, NEG_INF, work)
        cmax = v1[0] + v2[0]
        z = jnp.sum(jnp.where(cand >= tau, jnp.exp(cand - cmax), 0.0), axis=0, keepdims=True)
        cnt = jnp.zeros(s1.shape, F32)
        for a in range(PEER_TOPK):
            c_a = jnp.sum(jnp.where(v1[a] + v2s >= tau, 1.0, 0.0), axis=0, keepdims=True)
            cnt = jnp.where(s1 == v1[a], c_a, cnt)
        cnt_ref[h] = cnt
        a_ref[h] = jnp.exp(s1 - v1[0])
        bn = jnp.exp(s2 - v2[0]) * (0.5 / z)
        for c in range(s2.shape[1] // LANES):
            rk_ref[h, c] = _pack_pairs(rank2[:, c * LANES:(c + 1) * LANES])
            bn_ref[h, c] = _pack_pairs(bn[:, c * LANES:(c + 1) * LANES])


def _peer_select(x, wq, k1, k2):
    n = x.shape[0]
    tq = PEER_TQ
    hq = PEER_HEADS * PEER_QDIM
    big = pl.BlockSpec((PEER_HEADS, PEER_NKEYS, tq), lambda t: (0, 0, t))
    tiled = pl.BlockSpec((PEER_HEADS, tq // LANES, PEER_NKEYS // 2, LANES), lambda t: (0, t, 0, 0))
    keys = pl.BlockSpec((PEER_HEADS, PEER_NKEYS, PEER_QDIM // 2), lambda t: (0, 0, 0))
    shp = jax.ShapeDtypeStruct((PEER_HEADS, PEER_NKEYS, n), F32)
    shp_tiled = jax.ShapeDtypeStruct((PEER_HEADS, n // LANES, PEER_NKEYS // 2, LANES), jnp.uint32)
    return pl.pallas_call(
        _peer_sel_kernel,
        grid=(n // tq,),
        in_specs=[pl.BlockSpec((tq, D_MODEL), lambda t: (t, 0)),
                  pl.BlockSpec((D_MODEL, hq), lambda t: (0, 0)), keys, keys],
        out_specs=[big, big, tiled, tiled],
        out_shape=[shp, shp, shp_tiled, shp_tiled],
        compiler_params=_cparams(("parallel",)),
        name="peer_select",
    )(x, wq, k1, k2)


PEER_T = 512
PEER_EC = 2048
PEER_IPC = PEER_EC // PEER_NKEYS


def _peer_kernel(x_ref, u_ref, vt_ref, cnt_ref, a_ref, rk_ref, bn_ref, g_ref, b_ref,
                 o_ref, ob_ref, xt_ref, acc_ref, pre0_ref, pre1_ref, act0_ref, act1_ref):
    e = pl.program_id(1)
    ne = pl.num_programs(1) - 2
    t = x_ref.shape[0]

    def blocks(nrows):
        return [(slice(r, r + 512), slice(c, c + 256)) for r in range(0, nrows, 512) for c in range(0, t, 256)]

    def first_matmul(pre_ref):
        def piece(rows, cols):
            pre_ref[rows, cols] = jnp.dot(u_ref[rows, :], xt_ref[:, cols], preferred_element_type=F32)
        return [functools.partial(piece, r, c) for r, c in blocks(PEER_EC)]

    def second_matmul(act_ref):
        def piece(rows, cols):
            acc_ref[rows, cols] += jnp.dot(vt_ref[0, rows, :], act_ref[:, cols], preferred_element_type=F32)
        return [functools.partial(piece, r, c) for r, c in blocks(D_MODEL)]

    def gate(pre_ref, act_ref):
        def piece(ii, lt):
            rows = slice(ii * PEER_NKEYS, (ii + 1) * PEER_NKEYS)
            lanes = slice(lt * LANES, (lt + 1) * LANES)
            tile = (PEER_NKEYS, LANES)
            w = None
            for h in range(PEER_HEADS):
                cnt = jnp.broadcast_to(cnt_ref[h, ii:ii + 1, lanes].astype(BF16), tile)
                a = jnp.broadcast_to(a_ref[h, ii:ii + 1, lanes].astype(BF16), tile)
                hit = pltpu.bitcast(rk_ref[h, lt], BF16) < cnt
                term = jnp.where(hit, pltpu.bitcast(bn_ref[h, lt], BF16), jnp.zeros((), BF16)) * a
                w = term if w is None else w + term
            p = pre_ref[rows, lanes]
            gelu2 = p * (1.0 + lax.erf(p * np.float32(math.sqrt(0.5))))
            act_ref[rows, lanes] = gelu2.astype(BF16) * w
        return [functools.partial(piece, ii, lt) for ii in range(PEER_IPC) for lt in range(t // LANES)]

    def interleave(mxu_pieces, vpu_pieces):
        per = -(-len(vpu_pieces) // max(len(mxu_pieces), 1)) if vpu_pieces else 0
        for k, mp in enumerate(mxu_pieces):
            mp()
            for vp in vpu_pieces[k * per:(k + 1) * per]:
                vp()
        for vp in vpu_pieces[len(mxu_pieces) * per:]:
            vp()

    @pl.when(e == 0)
    def _():
        xt_ref[...] = x_ref[...].T.astype(BF16)
        acc_ref[...] = jnp.zeros_like(acc_ref)
        interleave(first_matmul(pre0_ref), [])

    @pl.when(e == 1)
    def _():
        interleave(first_matmul(pre1_ref), gate(pre0_ref, act0_ref))

    @pl.when(jnp.logical_and(jnp.logical_and(e >= 2, e < ne), e % 2 == 0))
    def _():
        interleave(first_matmul(pre0_ref) + second_matmul(act0_ref), gate(pre1_ref, act1_ref))

    @pl.when(jnp.logical_and(jnp.logical_and(e >= 2, e < ne), e % 2 == 1))
    def _():
        interleave(first_matmul(pre1_ref) + second_matmul(act1_ref), gate(pre0_ref, act0_ref))

    @pl.when(e == ne)
    def _():
        interleave(second_matmul(act0_ref), gate(pre1_ref, act1_ref))

    @pl.when(e == ne + 1)
    def _():
        interleave(second_matmul(act1_ref), [])
        y = acc_ref[...].T
        out = _layer_norm(np.float32(ALPHA) * x_ref[...] + y, g_ref[...], b_ref[...])
        o_ref[...] = out
        ob_ref[...] = out.astype(BF16)


def _peer_dense(x, u, vt, cnt, a, rk, bn, g, b):
    n = x.shape[0]
    t = PEER_T
    ne = u.shape[0] // PEER_EC
    assert ne % 2 == 0 and ne >= 2
    chunk = lambda e, lag: jnp.clip(e - lag, 0, ne - 1)
    rows = pl.BlockSpec((PEER_HEADS, PEER_IPC, t), lambda i, e: (0, chunk(e, 1), i))
    big = pl.BlockSpec((PEER_HEADS, t // LANES, PEER_NKEYS // 2, LANES), lambda i, e: (0, i, 0, 0))
    vec = pl.BlockSpec((1, D_MODEL), lambda i, e: (0, 0))
    return pl.pallas_call(
        _peer_kernel,
        grid=(n // t, ne + 2),
        in_specs=[pl.BlockSpec((t, D_MODEL), lambda i, e: (i, 0)),
                  pl.BlockSpec((PEER_EC, D_MODEL), lambda i, e: (chunk(e, 0), 0)),
                  pl.BlockSpec((1, D_MODEL, PEER_EC), lambda i, e: (chunk(e, 2), 0, 0)),
                  rows, rows, big, big, vec, vec],
        out_specs=[pl.BlockSpec((t, D_MODEL), lambda i, e: (i, 0))] * 2,
        out_shape=[jax.ShapeDtypeStruct((n, D_MODEL), F32), jax.ShapeDtypeStruct((n, D_MODEL), BF16)],
        scratch_shapes=[pltpu.VMEM((D_MODEL, t), BF16), pltpu.VMEM((D_MODEL, t), F32),
                        pltpu.VMEM((PEER_EC, t), F32), pltpu.VMEM((PEER_EC, t), F32),
                        pltpu.VMEM((PEER_EC, t), BF16), pltpu.VMEM((PEER_EC, t), BF16)],
        compiler_params=_cparams(("parallel", "arbitrary")),
        name="peer_dense",
    )(x, u, vt, cnt, a, rk, bn, g, b)


def _rope_tables(seq, dim):
    inv = ROPE_THETA ** (-jnp.arange(0, dim, 2, dtype=F32) / dim)
    ang = jnp.arange(seq, dtype=F32)[:, None] * inv[None, :]
    return jnp.cos(ang), jnp.sin(ang)


def _retention_tables():
    log_g = jnp.log(1.0 - 2.0 ** (-5.0 - jnp.arange(RET_HEADS, dtype=F32)))
    idx = jnp.arange(RET_CHUNK, dtype=F32)
    diff = idx[:, None] - idx[None, :]
    dmat = jnp.where(diff >= 0, jnp.exp(jnp.maximum(diff, 0.0)[None] * log_g[:, None, None]), 0.0)
    xi = jnp.exp((idx[None] + 1.0) * log_g[:, None])[..., None]
    zeta = jnp.exp((RET_CHUNK - 1.0 - idx[None]) * log_g[:, None])[..., None]
    gch = jnp.exp(RET_CHUNK * log_g)[:, None, None]
    return dmat, xi, zeta, gch


def _block_diag(w):
    g, i, o = w.shape
    eye = jnp.eye(g, dtype=w.dtype)
    return jnp.einsum('gio,gh->giho', w, eye).reshape(g * i, g * o)


def kernel(x, w_in, conv_w, conv_b, lru_wa, lru_ba, lru_wx, lru_bx, lru_lambda, w_branch, w_out,
           ln1_g, ln1_b, peer_wq, peer_k1, peer_k2, peer_u, peer_v, ln2_g, ln2_b):
    batch, seq, d = x.shape
    n = batch * seq
    depth = w_in.shape[0]
    cos_a, sin_a = _rope_tables(seq, ATT_E)
    cos_a2 = jnp.concatenate([cos_a, cos_a], axis=1)
    sin_a2 = jnp.concatenate([-sin_a, sin_a], axis=1)
    cos_r, sin_r = _rope_tables(seq, RET_DK)
    dmat, xi, zeta, gch = _retention_tables()

    xf = x.reshape(n, d)
    xb = xf.astype(BF16)
    for l in range(depth):
        w_att = w_in[l][:, :D_ATT].astype(BF16)
        w_rest = w_in[l][:, D_ATT:].astype(BF16)
        wg = jnp.concatenate([_block_diag(lru_wa[l]), _block_diag(lru_wx[l])], axis=1).astype(BF16)
        bg = jnp.concatenate([lru_ba[l], lru_bx[l]])[None, :]
        wb = w_branch[l].astype(BF16)
        wba, wbb, wbc = wb[:ATT_W], wb[ATT_W:ATT_W + d], wb[ATT_W + d:]
        wo = w_out[l].astype(BF16)
        wq = peer_wq[l].astype(BF16)
        k1 = peer_k1[l].astype(BF16)
        k2 = peer_k2[l].astype(BF16)
        u = peer_u[l].astype(BF16)
        vt = peer_v[l].reshape(-1, PEER_EC, d).transpose(0, 2, 1).astype(BF16)

        att3 = _matmul_heads(xb, w_att, batch, seq)
        rest = _matmul(xb, w_rest, BF16)
        rest3 = rest.reshape(batch, seq, D_REST)
        os_, ls_ = [], []
        for g, (_, dil) in enumerate(ATT_GROUPS):
            o, lse = _attention_group(att3, cos_a2, sin_a2, g, dil)
            os_.append(o)
            ls_.append(lse)
        yb = _lru(rest3, conv_w[l], conv_b[l][None, :], wg, bg, lru_lambda[l][None, :], BF16)
        yc = _retention(rest3, cos_r, sin_r, dmat, xi, zeta, gch, BF16)
        x1 = _merge(os_, ls_, yb.reshape(n, d), yc.reshape(n, d), rest, xf, wba, wbb, wbc, wo,
                    ln1_g[l][None, :], ln1_b[l][None, :])
        cnt, a, rk, bn = _peer_select(x1, wq, k1, k2)
        xf, xb = _peer_dense(x1, u, vt, cnt, a, rk, bn, ln2_g[l][None, :], ln2_b[l][None, :])
    return xf.reshape(batch, seq, d)
```

```python
import functools
import math

import numpy as np
import jax
import jax.numpy as jnp
from jax import lax
from jax.experimental import pallas as pl
from jax.experimental.pallas import tpu as pltpu

F32 = jnp.float32
BF16 = jnp.bfloat16

D_MODEL = 1024
DEPTH = 4
ROPE_THETA = 10000.0
LN_EPS = 1e-5
ATT_GROUPS = ((128, 1), (512, 4), (2048, 16))
ATT_HPG = 4
ATT_E = 128
ATT_BLK = 128
ATT_W = ATT_HPG * ATT_E
LRU_BLOCKS = 16
CONV_WIDTH = 4
LRU_C = 8.0
RET_HEADS = 4
RET_DK = 256
RET_CHUNK = 128
PEER_HEADS = 8
PEER_NKEYS = 128
PEER_QDIM = 256
PEER_TOPK = 16
ALPHA = (2 * DEPTH) ** 0.25

D_ATT = 3 * 1536
D_REST = 9 * 1024
COL_LX, COL_LG, COL_RQ, COL_RK, COL_RV, COL_RG, COL_GA, COL_GB, COL_GC = (2 * i for i in range(9))
COL_AQ, COL_AK, COL_AV = 0, 3, 6

VMEM_LIMIT = 56 * 1024 * 1024
LANES = 128
NEG_INF = float("-inf")


def _cparams(sem):
    return pltpu.CompilerParams(dimension_semantics=sem, vmem_limit_bytes=VMEM_LIMIT)


def _layer_norm(z, g, b):
    mu = jnp.mean(z, axis=-1, keepdims=True)
    zc = z - mu
    var = jnp.mean(zc * zc, axis=-1, keepdims=True)
    return zc * lax.rsqrt(var + LN_EPS) * g + b


def _gelu(x):
    return 0.5 * x * (1.0 + lax.erf(x * np.float32(math.sqrt(0.5))))


def _mm_kernel(x_ref, w_ref, o_ref):
    o_ref[...] = jnp.dot(x_ref[...], w_ref[...], preferred_element_type=F32).astype(o_ref.dtype)


def _matmul(x, w, out_dtype, tm=512, tn=1536):
    n, k = x.shape
    m = w.shape[1]
    return pl.pallas_call(
        _mm_kernel,
        grid=(m // tn, n // tm),
        in_specs=[pl.BlockSpec((tm, k), lambda j, i: (i, 0)),
                  pl.BlockSpec((k, tn), lambda j, i: (0, j))],
        out_specs=pl.BlockSpec((tm, tn), lambda j, i: (i, j)),
        out_shape=jax.ShapeDtypeStruct((n, m), out_dtype),
        compiler_params=_cparams(("parallel", "parallel")),
        name="in_proj",
    )(x, w)


def _mm_heads_kernel(x_ref, w_ref, o_ref):
    res = jnp.dot(x_ref[...], w_ref[...], preferred_element_type=F32)
    for h in range(o_ref.shape[1]):
        o_ref[0, h] = res[:, h * ATT_E:(h + 1) * ATT_E]


def _matmul_heads(x, w, batch, seq, tm=512, tn=1536):
    k = x.shape[1]
    heads = w.shape[1] // ATT_E
    hpt = tn // ATT_E
    spb = seq // tm
    return pl.pallas_call(
        _mm_heads_kernel,
        grid=(heads // hpt, batch * spb),
        in_specs=[pl.BlockSpec((tm, k), lambda j, i: (i, 0)),
                  pl.BlockSpec((k, tn), lambda j, i: (0, j))],
        out_specs=pl.BlockSpec((1, hpt, tm, ATT_E), lambda j, i: (i // spb, j, i % spb, 0)),
        out_shape=jax.ShapeDtypeStruct((batch, heads, seq, ATT_E), F32),
        compiler_params=_cparams(("parallel", "parallel")),
        name="in_proj_heads",
    )(x, w)


ATT_SPAN = 2048
ATT_GROUP = 4


def _att_kernel(dil, q_ref, kp_ref, kc_ref, vp_ref, vc_ref, cc_ref, sc_ref, cp_ref, sp_ref,
                o_ref, lse_ref, qs_ref, ks_ref, kps_ref):
    n = pl.program_id(2)
    halo = dil * ATT_BLK

    def rope(t, c, s):
        return t * c + pltpu.roll(t, ATT_E // 2, axis=1) * s

    cos_c, sin_c = cc_ref[...], sc_ref[...]
    qs_ref[...] = rope(q_ref[0, 0], cos_c, sin_c)
    ks_ref[...] = rope(kc_ref[0, 0], cos_c, sin_c)
    kps_ref[...] = rope(kp_ref[0, 0, ATT_SPAN - halo:, :],
                        cp_ref[ATT_SPAN - halo:, :], sp_ref[ATT_SPAN - halo:, :])

    qi = lax.broadcasted_iota(jnp.int32, (ATT_BLK, 2 * ATT_BLK), 0)
    kj = lax.broadcasted_iota(jnp.int32, (ATT_BLK, 2 * ATT_BLK), 1)
    delta = qi + ATT_BLK - kj
    inside = jnp.where(delta >= 0, jnp.where(delta <= ATT_BLK, 1, 0), 0)
    has_prev = jnp.where(n > 0, 1, 0)
    valid_mid = inside > 0
    valid_first = jnp.where(kj >= ATT_BLK, inside, inside * has_prev) > 0

    def rows(start):
        return pl.ds(start, ATT_BLK, stride=dil) if dil > 1 else pl.ds(start, ATT_BLK)

    blocks = [(r, jb) for r in range(dil) for jb in range(ATT_SPAN // halo)]
    for g0 in range(0, len(blocks), ATT_GROUP):
        group = blocks[g0:g0 + ATT_GROUP]
        scores, vcats, curs = [], [], []
        for r, jb in group:
            cur = rows(r + jb * halo)
            q = qs_ref[cur, :].astype(BF16)
            if jb == 0:
                prev = rows(ATT_SPAN - halo + r)
                kp, vp = kps_ref[rows(r), :], vp_ref[0, 0, prev, :]
            else:
                prev = rows(r + (jb - 1) * halo)
                kp, vp = ks_ref[prev, :], vc_ref[0, 0, prev, :]
            kcat = jnp.concatenate([kp, ks_ref[cur, :]], axis=0).astype(BF16)
            vcats.append(jnp.concatenate([vp, vc_ref[0, 0, cur, :]], axis=0).astype(BF16))
            s = lax.dot_general(q, kcat, (((1,), (1,)), ((), ())), preferred_element_type=F32)
            scores.append(jnp.where(valid_first if jb == 0 else valid_mid,
                                    s * np.float32(ATT_E ** -0.5), NEG_INF))
            curs.append(cur)
        s = jnp.concatenate(scores, axis=0)
        m = jnp.max(s, axis=-1, keepdims=True)
        p = jnp.exp(s - m)
        l = jnp.sum(p, axis=-1, keepdims=True)
        lse = jnp.broadcast_to(m + jnp.log(l), (len(group) * ATT_BLK, ATT_E))
        pb = p.astype(BF16)
        for k, cur in enumerate(curs):
            blk = slice(k * ATT_BLK, (k + 1) * ATT_BLK)
            o = jnp.dot(pb[blk], vcats[k], preferred_element_type=F32)
            o_ref[0, cur, :] = (o / l[blk]).astype(o_ref.dtype)
            lse_ref[0, cur, :] = lse[blk]


def _attention_group(proj3, cos_t, sin_t, g, dil):
    batch, _, seq, _ = proj3.shape
    assert seq % ATT_SPAN == 0 and ATT_SPAN % (dil * ATT_BLK) == 0
    head_slot = lambda col: col * (512 // ATT_E) + g * ATT_HPG

    def cur(col):
        return pl.BlockSpec((1, 1, ATT_SPAN, ATT_E), lambda b, h, n: (b, head_slot(col) + h, n, 0))

    def prev(col):
        return pl.BlockSpec((1, 1, ATT_SPAN, ATT_E),
                            lambda b, h, n: (b, head_slot(col) + h, jnp.maximum(n - 1, 0), 0))

    tab_c = pl.BlockSpec((ATT_SPAN, ATT_E), lambda b, h, n: (n, 0))
    tab_p = pl.BlockSpec((ATT_SPAN, ATT_E), lambda b, h, n: (jnp.maximum(n - 1, 0), 0))
    out_spec = pl.BlockSpec((1, ATT_SPAN, ATT_E), lambda b, h, n: (b, n, h))
    o, lse = pl.pallas_call(
        functools.partial(_att_kernel, dil),
        grid=(batch, ATT_HPG, seq // ATT_SPAN),
        in_specs=[cur(COL_AQ), prev(COL_AK), cur(COL_AK), prev(COL_AV), cur(COL_AV),
                  tab_c, tab_c, tab_p, tab_p],
        out_specs=[out_spec, out_spec],
        out_shape=[jax.ShapeDtypeStruct((batch, seq, ATT_W), F32)] * 2,
        scratch_shapes=[pltpu.VMEM((ATT_SPAN, ATT_E), F32), pltpu.VMEM((ATT_SPAN, ATT_E), F32),
                        pltpu.VMEM((dil * ATT_BLK, ATT_E), F32)],
        compiler_params=_cparams(("parallel", "parallel", "parallel")),
        name=f"dilated_attention_d{dil}",
    )(proj3, proj3, proj3, proj3, proj3, cos_t, sin_t, cos_t, sin_t)
    return o.reshape(batch * seq, ATT_W), lse.reshape(batch * seq, ATT_W)


LRU_TS = 256
LRU_HALO = 8


def _lru_kernel(lx_ref, lg_ref, cw_ref, cb_ref, wg_ref, bg_ref, lam_ref, y_ref, tail_ref, h_ref):
    s_idx = pl.program_id(1)
    ts, c = lx_ref.shape[1], lx_ref.shape[2]

    @pl.when(s_idx == 0)
    def _():
        tail_ref[...] = jnp.zeros_like(tail_ref)
        h_ref[...] = jnp.zeros_like(h_ref)

    ng = ts // LRU_HALO
    sub = lax.broadcasted_iota(jnp.int32, (ng, LRU_HALO, c), 1)
    x3 = lx_ref[0].astype(F32).reshape(ng, LRU_HALO, c)
    before = jnp.concatenate([tail_ref[...][None], x3[:ng - 1]], axis=0)
    xc3 = cb_ref[...] + x3 * cw_ref[CONV_WIDTH - 1:CONV_WIDTH, :]
    for k in range(1, CONV_WIDTH):
        shifted = jnp.where(sub >= k, pltpu.roll(x3, k, axis=1), pltpu.roll(before, k, axis=1))
        xc3 = xc3 + shifted * cw_ref[CONV_WIDTH - 1 - k:CONV_WIDTH - k, :]
    tail_ref[...] = x3[ng - 1]
    xc = xc3.reshape(ts, c)

    gates = jnp.dot(xc.astype(BF16), wg_ref[...], preferred_element_type=F32) + bg_ref[...]
    r = jax.nn.sigmoid(gates[:, :c])
    i = jax.nn.sigmoid(gates[:, c:])
    z = -lam_ref[...]
    softplus = jnp.maximum(z, 0.0) + jnp.log1p(jnp.exp(-jnp.abs(z)))
    log_a = (-LRU_C) * r * softplus
    a = jnp.exp(log_a)
    b = jnp.sqrt(1.0 - a * a) * (i * xc)

    a = a.reshape(ng, LRU_HALO, c)
    b = b.reshape(ng, LRU_HALO, c)
    k = 1
    while k < LRU_HALO:
        keep = sub >= k
        a_s = jnp.where(keep, pltpu.roll(a, k, axis=1), 1.0)
        b_s = jnp.where(keep, pltpu.roll(b, k, axis=1), 0.0)
        b = a * b_s + b
        a = a * a_s
        k *= 2
    carry = h_ref[...]
    groups = []
    for g in range(ng):
        hg = b[g] + a[g] * carry
        groups.append(hg)
        carry = hg[LRU_HALO - 1:LRU_HALO]
    h_ref[...] = carry
    h = jnp.concatenate(groups, axis=0)
    y_ref[0] = (h * _gelu(lg_ref[0].astype(F32))).astype(y_ref.dtype)


def _lru(proj3, conv_w, conv_b, wg, bg, lam, out_dtype):
    batch, seq, _ = proj3.shape
    c = D_MODEL
    row = lambda b, s: (0, 0)
    return pl.pallas_call(
        _lru_kernel,
        grid=(batch, seq // LRU_TS),
        in_specs=[pl.BlockSpec((1, LRU_TS, c), lambda b, s: (b, s, COL_LX // 2)),
                  pl.BlockSpec((1, LRU_TS, c), lambda b, s: (b, s, COL_LG // 2)),
                  pl.BlockSpec((CONV_WIDTH, c), row),
                  pl.BlockSpec((1, c), row),
                  pl.BlockSpec((c, 2 * c), row),
                  pl.BlockSpec((1, 2 * c), row),
                  pl.BlockSpec((1, c), row)],
        out_specs=pl.BlockSpec((1, LRU_TS, c), lambda b, s: (b, s, 0)),
        out_shape=jax.ShapeDtypeStruct((batch, seq, c), out_dtype),
        scratch_shapes=[pltpu.VMEM((LRU_HALO, c), F32), pltpu.VMEM((1, c), F32)],
        compiler_params=_cparams(("parallel", "arbitrary")),
        name="rg_lru",
    )(proj3, proj3, conv_w, conv_b, wg, bg, lam)


def _ret_kernel(q_ref, k_ref, v_ref, g_ref, cos_ref, sin_ref, dmat_ref, xi_ref, zeta_ref, gch_ref,
                y_ref, r_ref):
    @pl.when(pl.program_id(1) == 0)
    def _():
        r_ref[...] = jnp.zeros_like(r_ref)

    cos, sin = cos_ref[...], sin_ref[...]
    half = RET_DK // 2

    def rope(t):
        t1, t2 = t[:, :half], t[:, half:]
        return jnp.concatenate([t1 * cos - t2 * sin, t1 * sin + t2 * cos], axis=1)

    for h in range(RET_HEADS):
        sl = slice(h * RET_DK, (h + 1) * RET_DK)
        q = rope(q_ref[0, :, sl].astype(F32)).astype(BF16)
        k = rope(k_ref[0, :, sl].astype(F32)) * np.float32(RET_DK ** -0.5)
        v = v_ref[0, :, sl].astype(BF16)
        state = r_ref[h]
        qk = lax.dot_general(q, k.astype(BF16), (((1,), (1,)), ((), ())), preferred_element_type=F32)
        inner = jnp.dot((qk * dmat_ref[h]).astype(BF16), v, preferred_element_type=F32)
        cross = jnp.dot(q, state.astype(BF16), preferred_element_type=F32) * xi_ref[h]
        kz = (k * zeta_ref[h]).astype(BF16)
        r_ref[h] = gch_ref[h] * state + lax.dot_general(
            kz, v, (((0,), (0,)), ((), ())), preferred_element_type=F32)
        ret = inner + cross
        mu = jnp.mean(ret, axis=-1, keepdims=True)
        rc = ret - mu
        var = jnp.mean(rc * rc, axis=-1, keepdims=True)
        gate = g_ref[0, :, sl].astype(F32)
        y_ref[0, :, sl] = (gate * jax.nn.sigmoid(gate) * (rc * lax.rsqrt(var + LN_EPS))).astype(y_ref.dtype)


def _retention(proj3, cos_r, sin_r, dmat, xi, zeta, gch, out_dtype):
    batch, seq, _ = proj3.shape
    c = RET_HEADS * RET_DK
    blk = lambda col: pl.BlockSpec((1, RET_CHUNK, c), lambda b, n: (b, n, col // 2))
    tab = pl.BlockSpec((RET_CHUNK, RET_DK // 2), lambda b, n: (n, 0))
    const3 = lambda shape: pl.BlockSpec(shape, lambda b, n: (0, 0, 0))
    return pl.pallas_call(
        _ret_kernel,
        grid=(batch, seq // RET_CHUNK),
        in_specs=[blk(COL_RQ), blk(COL_RK), blk(COL_RV), blk(COL_RG), tab, tab,
                  const3((RET_HEADS, RET_CHUNK, RET_CHUNK)),
                  const3((RET_HEADS, RET_CHUNK, 1)),
                  const3((RET_HEADS, RET_CHUNK, 1)),
                  const3((RET_HEADS, 1, 1))],
        out_specs=pl.BlockSpec((1, RET_CHUNK, c), lambda b, n: (b, n, 0)),
        out_shape=jax.ShapeDtypeStruct((batch, seq, c), out_dtype),
        scratch_shapes=[pltpu.VMEM((RET_HEADS, RET_DK, RET_DK), F32)],
        compiler_params=_cparams(("parallel", "arbitrary")),
        name="retention",
    )(proj3, proj3, proj3, proj3, cos_r, sin_r, dmat, xi, zeta, gch)


MERGE_TM = 256


def _merge_kernel(o0, o1, o2, l0, l1, l2, yb_ref, yc_ref, ga_ref, gb_ref, gc_ref, x_ref,
                  wba_ref, wbb_ref, wbc_ref, wo_ref, g_ref, b_ref, out_ref):
    e0, e1, e2 = l0[...], l1[...], l2[...]
    mx = jnp.maximum(jnp.maximum(e0, e1), e2)
    w0, w1, w2 = jnp.exp(e0 - mx), jnp.exp(e1 - mx), jnp.exp(e2 - mx)
    ya = (w0 * o0[...] + w1 * o1[...] + w2 * o2[...]) / (w0 + w1 + w2)
    za = jnp.dot(ya.astype(BF16), wba_ref[...], preferred_element_type=F32)
    zb = jnp.dot(yb_ref[...], wbb_ref[...], preferred_element_type=F32)
    zc = jnp.dot(yc_ref[...], wbc_ref[...], preferred_element_type=F32)
    merged = (jax.nn.sigmoid(ga_ref[...].astype(F32)) * za
              + jax.nn.sigmoid(gb_ref[...].astype(F32)) * zb
              + jax.nn.sigmoid(gc_ref[...].astype(F32)) * zc)
    mix = jnp.dot(merged.astype(BF16), wo_ref[...], preferred_element_type=F32)
    out_ref[...] = _layer_norm(np.float32(ALPHA) * x_ref[...] + mix, g_ref[...], b_ref[...])


def _merge(os_, ls_, yb, yc, proj, x, wba, wbb, wbc, wo, g, b):
    n = x.shape[0]
    c = D_MODEL
    tm = MERGE_TM
    row512 = pl.BlockSpec((tm, ATT_W), lambda i: (i, 0))
    row = lambda col: pl.BlockSpec((tm, c), lambda i: (i, col))
    full = lambda shape: pl.BlockSpec(shape, lambda i: (0, 0))
    return pl.pallas_call(
        _merge_kernel,
        grid=(n // tm,),
        in_specs=[row512] * 6 + [row(0), row(0), row(COL_GA // 2), row(COL_GB // 2), row(COL_GC // 2),
                                 row(0), full((ATT_W, c)), full((c, c)), full((c, c)), full((c, c)),
                                 full((1, c)), full((1, c))],
        out_specs=row(0),
        out_shape=jax.ShapeDtypeStruct((n, c), F32),
        compiler_params=_cparams(("parallel",)),
        name="merge_out_ln",
    )(*os_, *ls_, yb, yc, proj, proj, proj, x, wba, wbb, wbc, wo, g, b)


PEER_TQ = 256


PEER_NO_RANK = 127.0


def _top16(s, want_rank=False):
    work = s
    vals = []
    rank = jnp.full(s.shape, PEER_NO_RANK, F32) if want_rank else None
    for r in range(PEER_TOPK):
        m = jnp.max(work, axis=0, keepdims=True)
        vals.append(m)
        top = work == m
        if want_rank:
            rank = jnp.where(top, np.float32(r), rank)
        work = jnp.where(top, NEG_INF, work)
    return vals, rank


def _pack_pairs(v):
    return pltpu.bitcast(v.astype(BF16), jnp.uint32)


def _peer_sel_kernel(x_ref, wq_ref, k1_ref, k2_ref, cnt_ref, a_ref, rk_ref, bn_ref):
    q = jnp.dot(x_ref[...].astype(BF16), wq_ref[...], preferred_element_type=F32).astype(BF16)
    half = PEER_QDIM // 2
    nt = (((1,), (1,)), ((), ()))
    for h in range(PEER_HEADS):
        q1 = q[:, h * PEER_QDIM:h * PEER_QDIM + half]
        q2 = q[:, h * PEER_QDIM + half:(h + 1) * PEER_QDIM]
        s1 = lax.dot_general(k1_ref[h], q1, nt, preferred_element_type=F32)
        s2 = lax.dot_general(k2_ref[h], q2, nt, preferred_element_type=F32)
        v1, _ = _top16(s1)
        v2, rank2 = _top16(s2, want_rank=True)
        v1s = jnp.concatenate(v1, axis=0)
        v2s = jnp.concatenate(v2, axis=0)
        cands = [v1s + v2[0]] + [v1s[0:8] + v2[b] for b in range(1, 8)] + [v1[0] + v2s[8:16]]
        cand = jnp.concatenate(cands, axis=0)
        work = cand
        for it in range(PEER_TOPK):
            tau = jnp.max(work, axis=0, keepdims=True)
            if it + 1 < PEER_TOPK:
                work = jnp.where(work == tau, NEG_INF, work)
        cmax = v1[0] + v2[0]
        z = jnp.sum(jnp.where(cand >= tau, jnp.exp(cand - cmax), 0.0), axis=0, keepdims=True)
        cnt = jnp.zeros(s1.shape, F32)
        for a in range(PEER_TOPK):
            c_a = jnp.sum(jnp.where(v1[a] + v2s >= tau, 1.0, 0.0), axis=0, keepdims=True)
            cnt = jnp.where(s1 == v1[a], c_a, cnt)
        cnt_ref[h] = cnt
        a_ref[h] = jnp.exp(s1 - v1[0])
        bn = jnp.exp(s2 - v2[0]) * (0.5 / z)
        for c in range(s2.shape[1] // LANES):
            rk_ref[h, c] = _pack_pairs(rank2[:, c * LANES:(c + 1) * LANES])
            bn_ref[h, c] = _pack_pairs(bn[:, c * LANES:(c + 1) * LANES])


def _peer_select(x, wq, k1, k2):
    n = x.shape[0]
    tq = PEER_TQ
    hq = PEER_HEADS * PEER_QDIM
    big = pl.BlockSpec((PEER_HEADS, PEER_NKEYS, tq), lambda t: (0, 0, t))
    tiled = pl.BlockSpec((PEER_HEADS, tq // LANES, PEER_NKEYS // 2, LANES), lambda t: (0, t, 0, 0))
    keys = pl.BlockSpec((PEER_HEADS, PEER_NKEYS, PEER_QDIM // 2), lambda t: (0, 0, 0))
    shp = jax.ShapeDtypeStruct((PEER_HEADS, PEER_NKEYS, n), F32)
    shp_tiled = jax.ShapeDtypeStruct((PEER_HEADS, n // LANES, PEER_NKEYS // 2, LANES), jnp.uint32)
    return pl.pallas_call(
        _peer_sel_kernel,
        grid=(n // tq,),
        in_specs=[pl.BlockSpec((tq, D_MODEL), lambda t: (t, 0)),
                  pl.BlockSpec((D_MODEL, hq), lambda t: (0, 0)), keys, keys],
        out_specs=[big, big, tiled, tiled],
        out_shape=[shp, shp, shp_tiled, shp_tiled],
        compiler_params=_cparams(("parallel",)),
        name="peer_select",
    )(x, wq, k1, k2)


PEER_T = 512
PEER_EC = 2048
PEER_IPC = PEER_EC // PEER_NKEYS


def _peer_kernel(x_ref, u_ref, vt_ref, cnt_ref, a_ref, rk_ref, bn_ref, g_ref, b_ref,
                 o_ref, ob_ref, xt_ref, acc_ref, pre0_ref, pre1_ref, act0_ref, act1_ref):
    e = pl.program_id(1)
    ne = pl.num_programs(1) - 2
    t = x_ref.shape[0]

    def blocks(nrows):
        return [(slice(r, r + 512), slice(c, c + 256)) for r in range(0, nrows, 512) for c in range(0, t, 256)]

    def first_matmul(pre_ref):
        def piece(rows, cols):
            pre_ref[rows, cols] = jnp.dot(u_ref[rows, :], xt_ref[:, cols], preferred_element_type=F32)
        return [functools.partial(piece, r, c) for r, c in blocks(PEER_EC)]

    def second_matmul(act_ref):
        def piece(rows, cols):
            acc_ref[rows, cols] += jnp.dot(vt_ref[0, rows, :], act_ref[:, cols], preferred_element_type=F32)
        return [functools.partial(piece, r, c) for r, c in blocks(D_MODEL)]

    def gate(pre_ref, act_ref):
        def piece(ii, lt):
            rows = slice(ii * PEER_NKEYS, (ii + 1) * PEER_NKEYS)
            lanes = slice(lt * LANES, (lt + 1) * LANES)
            tile = (PEER_NKEYS, LANES)
            w = None
            for h in range(PEER_HEADS):
                cnt = jnp.broadcast_to(cnt_ref[h, ii:ii + 1, lanes].astype(BF16), tile)
                a = jnp.broadcast_to(a_ref[h, ii:ii + 1, lanes].astype(BF16), tile)
                hit = pltpu.bitcast(rk_ref[h, lt], BF16) < cnt
                term = jnp.where(hit, pltpu.bitcast(bn_ref[h, lt], BF16), jnp.zeros((), BF16)) * a
                w = term if w is None else w + term
            p = pre_ref[rows, lanes]
            gelu2 = p * (1.0 + lax.erf(p * np.float32(math.sqrt(0.5))))
            act_ref[rows, lanes] = gelu2.astype(BF16) * w
        return [functools.partial(piece, ii, lt) for ii in range(PEER_IPC) for lt in range(t // LANES)]

    def interleave(mxu_pieces, vpu_pieces):
        per = -(-len(vpu_pieces) // max(len(mxu_pieces), 1)) if vpu_pieces else 0
        for k, mp in enumerate(mxu_pieces):
            mp()
            for vp in vpu_pieces[k * per:(k + 1) * per]:
                vp()
        for vp in vpu_pieces[len(mxu_pieces) * per:]:
            vp()

    @pl.when(e == 0)
    def _():
        xt_ref[...] = x_ref[...].T.astype(BF16)
        acc_ref[...] = jnp.zeros_like(acc_ref)
        interleave(first_matmul(pre0_ref), [])

    @pl.when(e == 1)
    def _():
        interleave(first_matmul(pre1_ref), gate(pre0_ref, act0_ref))

    @pl.when(jnp.logical_and(jnp.logical_and(e >= 2, e < ne), e % 2 == 0))
    def _():
        interleave(first_matmul(pre0_ref) + second_matmul(act0_ref), gate(pre1_ref, act1_ref))

    @pl.when(jnp.logical_and(jnp.logical_and(e >= 2, e < ne), e % 2 == 1))
    def _():
        interleave(first_matmul(pre1_ref) + second_matmul(act1_ref), gate(pre0_ref, act0_ref))

    @pl.when(e == ne)
    def _():
        interleave(second_matmul(act0_ref), gate(pre1_ref, act1_ref))

    @pl.when(e == ne + 1)
    def _():
        interleave(second_matmul(act1_ref), [])
        y = acc_ref[...].T
        out = _layer_norm(np.float32(ALPHA) * x_ref[...] + y, g_ref[...], b_ref[...])
        o_ref[...] = out
        ob_ref[...] = out.astype(BF16)


def _peer_dense(x, u, vt, cnt, a, rk, bn, g, b):
    n = x.shape[0]
    t = PEER_T
    ne = u.shape[0] // PEER_EC
    assert ne % 2 == 0 and ne >= 2
    chunk = lambda e, lag: jnp.clip(e - lag, 0, ne - 1)
    rows = pl.BlockSpec((PEER_HEADS, PEER_IPC, t), lambda i, e: (0, chunk(e, 1), i))
    big = pl.BlockSpec((PEER_HEADS, t // LANES, PEER_NKEYS // 2, LANES), lambda i, e: (0, i, 0, 0))
    vec = pl.BlockSpec((1, D_MODEL), lambda i, e: (0, 0))
    return pl.pallas_call(
        _peer_kernel,
        grid=(n // t, ne + 2),
        in_specs=[pl.BlockSpec((t, D_MODEL), lambda i, e: (i, 0)),
                  pl.BlockSpec((PEER_EC, D_MODEL), lambda i, e: (chunk(e, 0), 0)),
                  pl.BlockSpec((1, D_MODEL, PEER_EC), lambda i, e: (chunk(e, 2), 0, 0)),
                  rows, rows, big, big, vec, vec],
        out_specs=[pl.BlockSpec((t, D_MODEL), lambda i, e: (i, 0))] * 2,
        out_shape=[jax.ShapeDtypeStruct((n, D_MODEL), F32), jax.ShapeDtypeStruct((n, D_MODEL), BF16)],
        scratch_shapes=[pltpu.VMEM((D_MODEL, t), BF16), pltpu.VMEM((D_MODEL, t), F32),
                        pltpu.VMEM((PEER_EC, t), F32), pltpu.VMEM((PEER_EC, t), F32),
                        pltpu.VMEM((PEER_EC, t), BF16), pltpu.VMEM((PEER_EC, t), BF16)],
        compiler_params=_cparams(("parallel", "arbitrary")),
        name="peer_dense",
    )(x, u, vt, cnt, a, rk, bn, g, b)


def _rope_tables(seq, dim):
    inv = ROPE_THETA ** (-jnp.arange(0, dim, 2, dtype=F32) / dim)
    ang = jnp.arange(seq, dtype=F32)[:, None] * inv[None, :]
    return jnp.cos(ang), jnp.sin(ang)


def _retention_tables():
    log_g = jnp.log(1.0 - 2.0 ** (-5.0 - jnp.arange(RET_HEADS, dtype=F32)))
    idx = jnp.arange(RET_CHUNK, dtype=F32)
    diff = idx[:, None] - idx[None, :]
    dmat = jnp.where(diff >= 0, jnp.exp(jnp.maximum(diff, 0.0)[None] * log_g[:, None, None]), 0.0)
    xi = jnp.exp((idx[None] + 1.0) * log_g[:, None])[..., None]
    zeta = jnp.exp((RET_CHUNK - 1.0 - idx[None]) * log_g[:, None])[..., None]
    gch = jnp.exp(RET_CHUNK * log_g)[:, None, None]
    return dmat, xi, zeta, gch


def _block_diag(w):
    g, i, o = w.shape
    eye = jnp.eye(g, dtype=w.dtype)
    return jnp.einsum('gio,gh->giho', w, eye).reshape(g * i, g * o)


def kernel(x, w_in, conv_w, conv_b, lru_wa, lru_ba, lru_wx, lru_bx, lru_lambda, w_branch, w_out,
           ln1_g, ln1_b, peer_wq, peer_k1, peer_k2, peer_u, peer_v, ln2_g, ln2_b):
    batch, seq, d = x.shape
    n = batch * seq
    depth = w_in.shape[0]
    cos_a, sin_a = _rope_tables(seq, ATT_E)
    cos_a2 = jnp.concatenate([cos_a, cos_a], axis=1)
    sin_a2 = jnp.concatenate([-sin_a, sin_a], axis=1)
    cos_r, sin_r = _rope_tables(seq, RET_DK)
    dmat, xi, zeta, gch = _retention_tables()

    xf = x.reshape(n, d)
    xb = xf.astype(BF16)
    for l in range(depth):
        w_att = w_in[l][:, :D_ATT].astype(BF16)
        w_rest = w_in[l][:, D_ATT:].astype(BF16)
        wg = jnp.concatenate([_block_diag(lru_wa[l]), _block_diag(lru_wx[l])], axis=1).astype(BF16)
        bg = jnp.concatenate([lru_ba[l], lru_bx[l]])[None, :]
        wb = w_branch[l].astype(BF16)
        wba, wbb, wbc = wb[:ATT_W], wb[ATT_W:ATT_W + d], wb[ATT_W + d:]
        wo = w_out[l].astype(BF16)
        wq = peer_wq[l].astype(BF16)
        k1 = peer_k1[l].astype(BF16)
        k2 = peer_k2[l].astype(BF16)
        u = peer_u[l].astype(BF16)
        vt = peer_v[l].reshape(-1, PEER_EC, d).transpose(0, 2, 1).astype(BF16)

        att3 = _matmul_heads(xb, w_att, batch, seq)
        rest = _matmul(xb, w_rest, BF16)
        rest3 = rest.reshape(batch, seq, D_REST)
        os_, ls_ = [], []
        for g, (_, dil) in enumerate(ATT_GROUPS):
            o, lse = _attention_group(att3, cos_a2, sin_a2, g, dil)
            os_.append(o)
            ls_.append(lse)
        yb = _lru(rest3, conv_w[l], conv_b[l][None, :], wg, bg, lru_lambda[l][None, :], BF16)
        yc = _retention(rest3, cos_r, sin_r, dmat, xi, zeta, gch, BF16)
        x1 = _merge(os_, ls_, yb.reshape(n, d), yc.reshape(n, d), rest, xf, wba, wbb, wbc, wo,
                    ln1_g[l][None, :], ln1_b[l][None, :])
        cnt, a, rk, bn = _peer_select(x1, wq, k1, k2)
        xf, xb = _peer_dense(x1, u, vt, cnt, a, rk, bn, ln2_g[l][None, :], ln2_b[l][None, :])
    return xf.reshape(batch, seq, d)
```

```python
import functools
import math

import numpy as np
import jax
import jax.numpy as jnp
from jax import lax
from jax.experimental import pallas as pl
from jax.experimental.pallas import tpu as pltpu

F32 = jnp.float32
BF16 = jnp.bfloat16

D_MODEL = 1024
DEPTH = 4
ROPE_THETA = 10000.0
LN_EPS = 1e-5
ATT_GROUPS = ((128, 1), (512, 4), (2048, 16))
ATT_HPG = 4
ATT_E = 128
ATT_BLK = 128
ATT_W = ATT_HPG * ATT_E
LRU_BLOCKS = 16
CONV_WIDTH = 4
LRU_C = 8.0
RET_HEADS = 4
RET_DK = 256
RET_CHUNK = 128
RET_STEP_CHUNKS = 4
PEER_HEADS = 8
PEER_NKEYS = 128
PEER_QDIM = 256
PEER_TOPK = 16
ALPHA = (2 * DEPTH) ** 0.25

D_ATT = 3 * 1536
D_REST = 9 * 1024
COL_LX, COL_LG, COL_RQ, COL_RK, COL_RV, COL_RG, COL_GA, COL_GB, COL_GC = (2 * i for i in range(9))
COL_AQ, COL_AK, COL_AV = 0, 3, 6

VMEM_LIMIT = 56 * 1024 * 1024
LANES = 128
NEG_INF = float("-inf")


def _cparams(sem):
    return pltpu.CompilerParams(dimension_semantics=sem, vmem_limit_bytes=VMEM_LIMIT)


def _layer_norm(z, g, b):
    mu = jnp.mean(z, axis=-1, keepdims=True)
    zc = z - mu
    var = jnp.mean(zc * zc, axis=-1, keepdims=True)
    return zc * lax.rsqrt(var + LN_EPS) * g + b


def _gelu(x):
    return 0.5 * x * (1.0 + lax.erf(x * np.float32(math.sqrt(0.5))))


def _mm_kernel(x_ref, w_ref, o_ref):
    o_ref[...] = jnp.dot(x_ref[...], w_ref[...], preferred_element_type=F32).astype(o_ref.dtype)


def _matmul(x, w, out_dtype, tm=1024, tn=1536):
    n, k = x.shape
    m = w.shape[1]
    return pl.pallas_call(
        _mm_kernel,
        grid=(m // tn, n // tm),
        in_specs=[pl.BlockSpec((tm, k), lambda j, i: (i, 0)),
                  pl.BlockSpec((k, tn), lambda j, i: (0, j))],
        out_specs=pl.BlockSpec((tm, tn), lambda j, i: (i, j)),
        out_shape=jax.ShapeDtypeStruct((n, m), out_dtype),
        compiler_params=_cparams(("parallel", "parallel")),
        name="in_proj",
    )(x, w)


def _mm_heads_kernel(x_ref, w_ref, o_ref):
    res = jnp.dot(x_ref[...], w_ref[...], preferred_element_type=F32)
    for h in range(o_ref.shape[1]):
        o_ref[0, h] = res[:, h * ATT_E:(h + 1) * ATT_E]


def _matmul_heads(x, w, batch, seq, tm=1024, tn=1536):
    k = x.shape[1]
    heads = w.shape[1] // ATT_E
    hpt = tn // ATT_E
    spb = seq // tm
    return pl.pallas_call(
        _mm_heads_kernel,
        grid=(heads // hpt, batch * spb),
        in_specs=[pl.BlockSpec((tm, k), lambda j, i: (i, 0)),
                  pl.BlockSpec((k, tn), lambda j, i: (0, j))],
        out_specs=pl.BlockSpec((1, hpt, tm, ATT_E), lambda j, i: (i // spb, j, i % spb, 0)),
        out_shape=jax.ShapeDtypeStruct((batch, heads, seq, ATT_E), F32),
        compiler_params=_cparams(("parallel", "parallel")),
        name="in_proj_heads",
    )(x, w)


ATT_SPAN = 2048
ATT_GROUP = 4


def _att_kernel(dil, q_ref, kp_ref, kc_ref, vp_ref, vc_ref, cc_ref, sc_ref, cp_ref, sp_ref,
                o_ref, lse_ref, qs_ref, ks_ref, kps_ref):
    n = pl.program_id(2)
    halo = dil * ATT_BLK

    def rope(t, c, s):
        return t * c + pltpu.roll(t, ATT_E // 2, axis=1) * s

    cos_c, sin_c = cc_ref[...], sc_ref[...]
    qs_ref[...] = rope(q_ref[0, 0], cos_c, sin_c)
    ks_ref[...] = rope(kc_ref[0, 0], cos_c, sin_c)
    kps_ref[...] = rope(kp_ref[0, 0, ATT_SPAN - halo:, :],
                        cp_ref[ATT_SPAN - halo:, :], sp_ref[ATT_SPAN - halo:, :])

    qi = lax.broadcasted_iota(jnp.int32, (ATT_BLK, 2 * ATT_BLK), 0)
    kj = lax.broadcasted_iota(jnp.int32, (ATT_BLK, 2 * ATT_BLK), 1)
    delta = qi + ATT_BLK - kj
    inside = jnp.where(delta >= 0, jnp.where(delta <= ATT_BLK, 1, 0), 0)
    has_prev = jnp.where(n > 0, 1, 0)
    valid_mid = inside > 0
    valid_first = jnp.where(kj >= ATT_BLK, inside, inside * has_prev) > 0

    def rows(start):
        return pl.ds(start, ATT_BLK, stride=dil) if dil > 1 else pl.ds(start, ATT_BLK)

    blocks = [(r, jb) for r in range(dil) for jb in range(ATT_SPAN // halo)]
    for g0 in range(0, len(blocks), ATT_GROUP):
        group = blocks[g0:g0 + ATT_GROUP]
        scores, vcats, curs = [], [], []
        for r, jb in group:
            cur = rows(r + jb * halo)
            q = qs_ref[cur, :].astype(BF16)
            if jb == 0:
                prev = rows(ATT_SPAN - halo + r)
                kp, vp = kps_ref[rows(r), :], vp_ref[0, 0, prev, :]
            else:
                prev = rows(r + (jb - 1) * halo)
                kp, vp = ks_ref[prev, :], vc_ref[0, 0, prev, :]
            kcat = jnp.concatenate([kp, ks_ref[cur, :]], axis=0).astype(BF16)
            vcats.append(jnp.concatenate([vp, vc_ref[0, 0, cur, :]], axis=0).astype(BF16))
            s = lax.dot_general(q, kcat, (((1,), (1,)), ((), ())), preferred_element_type=F32)
            scores.append(jnp.where(valid_first if jb == 0 else valid_mid,
                                    s * np.float32(ATT_E ** -0.5), NEG_INF))
            curs.append(cur)
        s = jnp.concatenate(scores, axis=0)
        m = jnp.max(s, axis=-1, keepdims=True)
        p = jnp.exp(s - m)
        l = jnp.sum(p, axis=-1, keepdims=True)
        lse = jnp.broadcast_to(m + jnp.log(l), (len(group) * ATT_BLK, ATT_E))
        pb = p.astype(BF16)
        for k, cur in enumerate(curs):
            blk = slice(k * ATT_BLK, (k + 1) * ATT_BLK)
            o = jnp.dot(pb[blk], vcats[k], preferred_element_type=F32)
            o_ref[0, cur, :] = (o / l[blk]).astype(o_ref.dtype)
            lse_ref[0, cur, :] = lse[blk]


def _attention_group(proj3, cos_t, sin_t, g, dil):
    batch, _, seq, _ = proj3.shape
    assert seq % ATT_SPAN == 0 and ATT_SPAN % (dil * ATT_BLK) == 0
    head_slot = lambda col: col * (512 // ATT_E) + g * ATT_HPG

    def cur(col):
        return pl.BlockSpec((1, 1, ATT_SPAN, ATT_E), lambda b, h, n: (b, head_slot(col) + h, n, 0))

    def prev(col):
        return pl.BlockSpec((1, 1, ATT_SPAN, ATT_E),
                            lambda b, h, n: (b, head_slot(col) + h, jnp.maximum(n - 1, 0), 0))

    tab_c = pl.BlockSpec((ATT_SPAN, ATT_E), lambda b, h, n: (n, 0))
    tab_p = pl.BlockSpec((ATT_SPAN, ATT_E), lambda b, h, n: (jnp.maximum(n - 1, 0), 0))
    out_spec = pl.BlockSpec((1, ATT_SPAN, ATT_E), lambda b, h, n: (b, n, h))
    o, lse = pl.pallas_call(
        functools.partial(_att_kernel, dil),
        grid=(batch, ATT_HPG, seq // ATT_SPAN),
        in_specs=[cur(COL_AQ), prev(COL_AK), cur(COL_AK), prev(COL_AV), cur(COL_AV),
                  tab_c, tab_c, tab_p, tab_p],
        out_specs=[out_spec, out_spec],
        out_shape=[jax.ShapeDtypeStruct((batch, seq, ATT_W), F32)] * 2,
        scratch_shapes=[pltpu.VMEM((ATT_SPAN, ATT_E), F32), pltpu.VMEM((ATT_SPAN, ATT_E), F32),
                        pltpu.VMEM((dil * ATT_BLK, ATT_E), F32)],
        compiler_params=_cparams(("parallel", "parallel", "parallel")),
        name=f"dilated_attention_d{dil}",
    )(proj3, proj3, proj3, proj3, proj3, cos_t, sin_t, cos_t, sin_t)
    return o.reshape(batch * seq, ATT_W), lse.reshape(batch * seq, ATT_W)


LRU_TS = 256
LRU_HALO = 8


def _lru_kernel(lx_ref, lg_ref, cw_ref, cb_ref, wg_ref, bg_ref, lam_ref, y_ref, tail_ref, h_ref):
    s_idx = pl.program_id(1)
    ts, c = lx_ref.shape[1], lx_ref.shape[2]

    @pl.when(s_idx == 0)
    def _():
        tail_ref[...] = jnp.zeros_like(tail_ref)
        h_ref[...] = jnp.zeros_like(h_ref)

    ng = ts // LRU_HALO
    sub = lax.broadcasted_iota(jnp.int32, (ng, LRU_HALO, c), 1)
    x3 = lx_ref[0].astype(F32).reshape(ng, LRU_HALO, c)
    before = jnp.concatenate([tail_ref[...][None], x3[:ng - 1]], axis=0)
    xc3 = cb_ref[...] + x3 * cw_ref[CONV_WIDTH - 1:CONV_WIDTH, :]
    for k in range(1, CONV_WIDTH):
        shifted = jnp.where(sub >= k, pltpu.roll(x3, k, axis=1), pltpu.roll(before, k, axis=1))
        xc3 = xc3 + shifted * cw_ref[CONV_WIDTH - 1 - k:CONV_WIDTH - k, :]
    tail_ref[...] = x3[ng - 1]
    xc = xc3.reshape(ts, c)

    gates = jnp.dot(xc.astype(BF16), wg_ref[...], preferred_element_type=F32) + bg_ref[...]
    r = jax.nn.sigmoid(gates[:, :c])
    i = jax.nn.sigmoid(gates[:, c:])
    z = -lam_ref[...]
    softplus = jnp.maximum(z, 0.0) + jnp.log1p(jnp.exp(-jnp.abs(z)))
    log_a = (-LRU_C) * r * softplus
    a = jnp.exp(log_a)
    b = jnp.sqrt(1.0 - a * a) * (i * xc)

    a = a.reshape(ng, LRU_HALO, c)
    b = b.reshape(ng, LRU_HALO, c)
    k = 1
    while k < LRU_HALO:
        keep = sub >= k
        a_s = jnp.where(keep, pltpu.roll(a, k, axis=1), 1.0)
        b_s = jnp.where(keep, pltpu.roll(b, k, axis=1), 0.0)
        b = a * b_s + b
        a = a * a_s
        k *= 2
    carry = h_ref[...]
    groups = []
    for g in range(ng):
        hg = b[g] + a[g] * carry
        groups.append(hg)
        carry = hg[LRU_HALO - 1:LRU_HALO]
    h_ref[...] = carry
    h = jnp.concatenate(groups, axis=0)
    y_ref[0] = (h * _gelu(lg_ref[0].astype(F32))).astype(y_ref.dtype)


def _lru(proj3, conv_w, conv_b, wg, bg, lam, out_dtype):
    batch, seq, _ = proj3.shape
    c = D_MODEL
    row = lambda b, s: (0, 0)
    return pl.pallas_call(
        _lru_kernel,
        grid=(batch, seq // LRU_TS),
        in_specs=[pl.BlockSpec((1, LRU_TS, c), lambda b, s: (b, s, COL_LX // 2)),
                  pl.BlockSpec((1, LRU_TS, c), lambda b, s: (b, s, COL_LG // 2)),
                  pl.BlockSpec((CONV_WIDTH, c), row),
                  pl.BlockSpec((1, c), row),
                  pl.BlockSpec((c, 2 * c), row),
                  pl.BlockSpec((1, 2 * c), row),
                  pl.BlockSpec((1, c), row)],
        out_specs=pl.BlockSpec((1, LRU_TS, c), lambda b, s: (b, s, 0)),
        out_shape=jax.ShapeDtypeStruct((batch, seq, c), out_dtype),
        scratch_shapes=[pltpu.VMEM((LRU_HALO, c), F32), pltpu.VMEM((1, c), F32)],
        compiler_params=_cparams(("parallel", "arbitrary")),
        name="rg_lru",
    )(proj3, proj3, conv_w, conv_b, wg, bg, lam)


def _ret_kernel(q_ref, k_ref, v_ref, g_ref, cos_ref, sin_ref, dmat_ref, xi_ref, zeta_ref, gch_ref,
                y_ref, r_ref):
    @pl.when(pl.program_id(1) == 0)
    def _():
        r_ref[...] = jnp.zeros_like(r_ref)

    half = RET_DK // 2

    def rope(t, cos, sin):
        t1, t2 = t[:, :half], t[:, half:]
        return jnp.concatenate([t1 * cos - t2 * sin, t1 * sin + t2 * cos], axis=1)

    for h in range(RET_HEADS):
        sl = slice(h * RET_DK, (h + 1) * RET_DK)
        state = r_ref[h]
        for ci in range(RET_STEP_CHUNKS):
            rows = slice(ci * RET_CHUNK, (ci + 1) * RET_CHUNK)
            cos, sin = cos_ref[rows, :], sin_ref[rows, :]
            q = rope(q_ref[0, rows, sl].astype(F32), cos, sin).astype(BF16)
            k = rope(k_ref[0, rows, sl].astype(F32), cos, sin) * np.float32(RET_DK ** -0.5)
            v = v_ref[0, rows, sl]
            qk = lax.dot_general(q, k.astype(BF16), (((1,), (1,)), ((), ())), preferred_element_type=F32)
            inner = jnp.dot((qk * dmat_ref[h]).astype(BF16), v, preferred_element_type=F32)
            cross = jnp.dot(q, state.astype(BF16), preferred_element_type=F32) * xi_ref[h]
            kz = (k * zeta_ref[h]).astype(BF16)
            state = gch_ref[h] * state + lax.dot_general(
                kz, v, (((0,), (0,)), ((), ())), preferred_element_type=F32)
            ret = inner + cross
            mu = jnp.mean(ret, axis=-1, keepdims=True)
            rc = ret - mu
            var = jnp.mean(rc * rc, axis=-1, keepdims=True)
            gate = g_ref[0, rows, sl].astype(F32)
            y_ref[0, rows, sl] = (gate * jax.nn.sigmoid(gate) * (rc * lax.rsqrt(var + LN_EPS))).astype(y_ref.dtype)
        r_ref[h] = state


def _retention(proj3, cos_r, sin_r, dmat, xi, zeta, gch, out_dtype):
    batch, seq, _ = proj3.shape
    c = RET_HEADS * RET_DK
    rows = RET_STEP_CHUNKS * RET_CHUNK
    blk = lambda col: pl.BlockSpec((1, rows, c), lambda b, n: (b, n, col // 2))
    tab = pl.BlockSpec((rows, RET_DK // 2), lambda b, n: (n, 0))
    const3 = lambda shape: pl.BlockSpec(shape, lambda b, n: (0, 0, 0))
    return pl.pallas_call(
        _ret_kernel,
        grid=(batch, seq // rows),
        in_specs=[blk(COL_RQ), blk(COL_RK), blk(COL_RV), blk(COL_RG), tab, tab,
                  const3((RET_HEADS, RET_CHUNK, RET_CHUNK)),
                  const3((RET_HEADS, RET_CHUNK, 1)),
                  const3((RET_HEADS, RET_CHUNK, 1)),
                  const3((RET_HEADS, 1, 1))],
        out_specs=pl.BlockSpec((1, rows, c), lambda b, n: (b, n, 0)),
        out_shape=jax.ShapeDtypeStruct((batch, seq, c), out_dtype),
        scratch_shapes=[pltpu.VMEM((RET_HEADS, RET_DK, RET_DK), F32)],
        compiler_params=_cparams(("parallel", "arbitrary")),
        name="retention",
    )(proj3, proj3, proj3, proj3, cos_r, sin_r, dmat, xi, zeta, gch)


MERGE_TM = 512


def _merge_kernel(o0, o1, o2, l0, l1, l2, yb_ref, yc_ref, ga_ref, gb_ref, gc_ref, x_ref,
                  wba_ref, wbb_ref, wbc_ref, wo_ref, g_ref, b_ref, out_ref):
    e0, e1, e2 = l0[...], l1[...], l2[...]
    mx = jnp.maximum(jnp.maximum(e0, e1), e2)
    w0, w1, w2 = jnp.exp(e0 - mx), jnp.exp(e1 - mx), jnp.exp(e2 - mx)
    ya = (w0 * o0[...] + w1 * o1[...] + w2 * o2[...]) / (w0 + w1 + w2)
    za = jnp.dot(ya.astype(BF16), wba_ref[...], preferred_element_type=F32)
    zb = jnp.dot(yb_ref[...], wbb_ref[...], preferred_element_type=F32)
    zc = jnp.dot(yc_ref[...], wbc_ref[...], preferred_element_type=F32)
    merged = (jax.nn.sigmoid(ga_ref[...].astype(F32)) * za
              + jax.nn.sigmoid(gb_ref[...].astype(F32)) * zb
              + jax.nn.sigmoid(gc_ref[...].astype(F32)) * zc)
    mix = jnp.dot(merged.astype(BF16), wo_ref[...], preferred_element_type=F32)
    out_ref[...] = _layer_norm(np.float32(ALPHA) * x_ref[...] + mix, g_ref[...], b_ref[...])


def _merge(os_, ls_, yb, yc, proj, x, wba, wbb, wbc, wo, g, b):
    n = x.shape[0]
    c = D_MODEL
    tm = MERGE_TM
    row512 = pl.BlockSpec((tm, ATT_W), lambda i: (i, 0))
    row = lambda col: pl.BlockSpec((tm, c), lambda i: (i, col))
    full = lambda shape: pl.BlockSpec(shape, lambda i: (0, 0))
    return pl.pallas_call(
        _merge_kernel,
        grid=(n // tm,),
        in_specs=[row512] * 6 + [row(0), row(0), row(COL_GA // 2), row(COL_GB // 2), row(COL_GC // 2),
                                 row(0), full((ATT_W, c)), full((c, c)), full((c, c)), full((c, c)),
                                 full((1, c)), full((1, c))],
        out_specs=row(0),
        out_shape=jax.ShapeDtypeStruct((n, c), F32),
        compiler_params=_cparams(("parallel",)),
        name="merge_out_ln",
    )(*os_, *ls_, yb, yc, proj, proj, proj, x, wba, wbb, wbc, wo, g, b)


PEER_TQ = 256


PEER_NO_RANK = 127.0


def _oddeven_merge(lo, hi, r):
    step = r * 2
    if step < hi - lo:
        yield from _oddeven_merge(lo, hi, step)
        yield from _oddeven_merge(lo + r, hi, step)
        yield from [(i, i + r) for i in range(lo + r, hi - r, step)]
    else:
        yield (lo, lo + r)


def _oddeven_merge_sort(lo, hi):
    if hi - lo >= 1:
        mid = lo + (hi - lo) // 2
        yield from _oddeven_merge_sort(lo, mid)
        yield from _oddeven_merge_sort(mid + 1, hi)
        yield from _oddeven_merge(lo, hi, 1)


SORT16 = tuple(_oddeven_merge_sort(0, PEER_TOPK - 1))
BITONIC16 = tuple((i, i + d) for d in (8, 4, 2, 1) for i in range(PEER_TOPK) if not i & d)
SUBLANES = 8


def _sorted_desc(g, comparators):
    g = list(g)
    for i, j in comparators:
        g[i], g[j] = jnp.maximum(g[i], g[j]), jnp.minimum(g[i], g[j])
    return g


def _top16(s, want_rank=False):
    assert s.shape[0] == PEER_TOPK * SUBLANES
    g = [s[r * SUBLANES:(r + 1) * SUBLANES, :] for r in range(PEER_TOPK)]
    g = _sorted_desc(g, SORT16)
    shift = SUBLANES // 2
    while shift >= 1:
        moved = [pltpu.roll(x, shift, axis=0) for x in g]
        g = _sorted_desc([jnp.maximum(g[k], moved[PEER_TOPK - 1 - k]) for k in range(PEER_TOPK)], BITONIC16)
        shift //= 2
    vals = [x[0:1, :] for x in g]
    rank = None
    if want_rank:
        rank = jnp.full(s.shape, PEER_NO_RANK, F32)
        for r in range(PEER_TOPK):
            rank = jnp.where(s == vals[r], np.float32(r), rank)
    return vals, rank


def _pack_pairs(v):
    return pltpu.bitcast(v.astype(BF16), jnp.uint32)


def _peer_sel_kernel(x_ref, wq_ref, k1_ref, k2_ref, cnt_ref, a_ref, rk_ref, bn_ref):
    q = jnp.dot(x_ref[...].astype(BF16), wq_ref[...], preferred_element_type=F32).astype(BF16)
    half = PEER_QDIM // 2
    nt = (((1,), (1,)), ((), ()))
    for h in range(PEER_HEADS):
        q1 = q[:, h * PEER_QDIM:h * PEER_QDIM + half]
        q2 = q[:, h * PEER_QDIM + half:(h + 1) * PEER_QDIM]
        s1 = lax.dot_general(k1_ref[h], q1, nt, preferred_element_type=F32)
        s2 = lax.dot_general(k2_ref[h], q2, nt, preferred_element_type=F32)
        v1, _ = _top16(s1)
        v2, rank2 = _top16(s2, want_rank=True)
        v1s = jnp.concatenate(v1, axis=0)
        v2s = jnp.concatenate(v2, axis=0)
        cands = [v1s + v2[0]] + [v1s[0:8] + v2[b] for b in range(1, 8)] + [v1[0] + v2s[8:16]]
        cand = jnp.concatenate(cands, axis=0)
        work = cand
        for it in range(PEER_TOPK):
            tau = jnp.max(work, axis=0, keepdims=True)
            if it + 1 < PEER_TOPK:
                work = jnp.where(work == tau, NEG_INF, work)
        cmax = v1[0] + v2[0]
        z = jnp.sum(jnp.where(cand >= tau, jnp.exp(cand - cmax), 0.0), axis=0, keepdims=True)
        cnt = jnp.zeros(s1.shape, F32)
        for a in range(PEER_TOPK):
            c_a = jnp.sum(jnp.where(v1[a] + v2s >= tau, 1.0, 0.0), axis=0, keepdims=True)
            cnt = jnp.where(s1 == v1[a], c_a, cnt)
        cnt_ref[h] = cnt
        a_ref[h] = jnp.exp(s1 - v1[0])
        bn = jnp.exp(s2 - v2[0]) * (0.5 / z)
        for c in range(s2.shape[1] // LANES):
            rk_ref[h, c] = _pack_pairs(rank2[:, c * LANES:(c + 1) * LANES])
            bn_ref[h, c] = _pack_pairs(bn[:, c * LANES:(c + 1) * LANES])


def _peer_select(x, wq, k1, k2):
    n = x.shape[0]
    tq = PEER_TQ
    hq = PEER_HEADS * PEER_QDIM
    big = pl.BlockSpec((PEER_HEADS, PEER_NKEYS, tq), lambda t: (0, 0, t))
    tiled = pl.BlockSpec((PEER_HEADS, tq // LANES, PEER_NKEYS // 2, LANES), lambda t: (0, t, 0, 0))
    keys = pl.BlockSpec((PEER_HEADS, PEER_NKEYS, PEER_QDIM // 2), lambda t: (0, 0, 0))
    shp = jax.ShapeDtypeStruct((PEER_HEADS, PEER_NKEYS, n), F32)
    shp_tiled = jax.ShapeDtypeStruct((PEER_HEADS, n // LANES, PEER_NKEYS // 2, LANES), jnp.uint32)
    return pl.pallas_call(
        _peer_sel_kernel,
        grid=(n // tq,),
        in_specs=[pl.BlockSpec((tq, D_MODEL), lambda t: (t, 0)),
                  pl.BlockSpec((D_MODEL, hq), lambda t: (0, 0)), keys, keys],
        out_specs=[big, big, tiled, tiled],
        out_shape=[shp, shp, shp_tiled, shp_tiled],
        compiler_params=_cparams(("parallel",)),
        name="peer_select",
    )(x, wq, k1, k2)


PEER_T = 512
PEER_EC = 2048
PEER_IPC = PEER_EC // PEER_NKEYS


def _peer_kernel(x_ref, u_ref, vt_ref, cnt_ref, a_ref, rk_ref, bn_ref, g_ref, b_ref,
                 o_ref, ob_ref, xt_ref, acc_ref, pre0_ref, pre1_ref, act0_ref, act1_ref):
    e = pl.program_id(1)
    ne = pl.num_programs(1) - 2
    t = x_ref.shape[0]

    def blocks(nrows):
        return [(slice(r, r + 512), slice(c, c + 256)) for r in range(0, nrows, 512) for c in range(0, t, 256)]

    def first_matmul(pre_ref):
        def piece(rows, cols):
            pre_ref[rows, cols] = jnp.dot(u_ref[rows, :], xt_ref[:, cols], preferred_element_type=F32)
        return [functools.partial(piece, r, c) for r, c in blocks(PEER_EC)]

    def second_matmul(act_ref):
        def piece(rows, cols):
            acc_ref[rows, cols] += jnp.dot(vt_ref[0, rows, :], act_ref[:, cols], preferred_element_type=F32)
        return [functools.partial(piece, r, c) for r, c in blocks(D_MODEL)]

    def gate(pre_ref, act_ref):
        def piece(ii, lt):
            rows = slice(ii * PEER_NKEYS, (ii + 1) * PEER_NKEYS)
            lanes = slice(lt * LANES, (lt + 1) * LANES)
            tile = (PEER_NKEYS, LANES)
            w = None
            for h in range(PEER_HEADS):
                cnt = jnp.broadcast_to(cnt_ref[h, ii:ii + 1, lanes].astype(BF16), tile)
                a = jnp.broadcast_to(a_ref[h, ii:ii + 1, lanes].astype(BF16), tile)
                hit = pltpu.bitcast(rk_ref[h, lt], BF16) < cnt
                term = jnp.where(hit, pltpu.bitcast(bn_ref[h, lt], BF16), jnp.zeros((), BF16)) * a
                w = term if w is None else w + term
            p = pre_ref[rows, lanes]
            gelu2 = p * (1.0 + lax.erf(p * np.float32(math.sqrt(0.5))))
            act_ref[rows, lanes] = gelu2.astype(BF16) * w
        return [functools.partial(piece, ii, lt) for ii in range(PEER_IPC) for lt in range(t // LANES)]

    def interleave(mxu_pieces, vpu_pieces):
        per = -(-len(vpu_pieces) // max(len(mxu_pieces), 1)) if vpu_pieces else 0
        for k, mp in enumerate(mxu_pieces):
            mp()
            for vp in vpu_pieces[k * per:(k + 1) * per]:
                vp()
        for vp in vpu_pieces[len(mxu_pieces) * per:]:
            vp()

    @pl.when(e == 0)
    def _():
        xt_ref[...] = x_ref[...].T.astype(BF16)
        acc_ref[...] = jnp.zeros_like(acc_ref)
        interleave(first_matmul(pre0_ref), [])

    @pl.when(e == 1)
    def _():
        interleave(first_matmul(pre1_ref), gate(pre0_ref, act0_ref))

    @pl.when(jnp.logical_and(jnp.logical_and(e >= 2, e < ne), e % 2 == 0))
    def _():
        interleave(first_matmul(pre0_ref) + second_matmul(act0_ref), gate(pre1_ref, act1_ref))

    @pl.when(jnp.logical_and(jnp.logical_and(e >= 2, e < ne), e % 2 == 1))
    def _():
        interleave(first_matmul(pre1_ref) + second_matmul(act1_ref), gate(pre0_ref, act0_ref))

    @pl.when(e == ne)
    def _():
        interleave(second_matmul(act0_ref), gate(pre1_ref, act1_ref))

    @pl.when(e == ne + 1)
    def _():
        interleave(second_matmul(act1_ref), [])
        y = acc_ref[...].T
        out = _layer_norm(np.float32(ALPHA) * x_ref[...] + y, g_ref[...], b_ref[...])
        o_ref[...] = out
        ob_ref[...] = out.astype(BF16)


def _peer_dense(x, u, vt, cnt, a, rk, bn, g, b):
    n = x.shape[0]
    t = PEER_T
    ne = u.shape[0] // PEER_EC
    assert ne % 2 == 0 and ne >= 2
    chunk = lambda e, lag: jnp.clip(e - lag, 0, ne - 1)
    rows = pl.BlockSpec((PEER_HEADS, PEER_IPC, t), lambda i, e: (0, chunk(e, 1), i))
    big = pl.BlockSpec((PEER_HEADS, t // LANES, PEER_NKEYS // 2, LANES), lambda i, e: (0, i, 0, 0))
    vec = pl.BlockSpec((1, D_MODEL), lambda i, e: (0, 0))
    return pl.pallas_call(
        _peer_kernel,
        grid=(n // t, ne + 2),
        in_specs=[pl.BlockSpec((t, D_MODEL), lambda i, e: (i, 0)),
                  pl.BlockSpec((PEER_EC, D_MODEL), lambda i, e: (chunk(e, 0), 0)),
                  pl.BlockSpec((1, D_MODEL, PEER_EC), lambda i, e: (chunk(e, 2), 0, 0)),
                  rows, rows, big, big, vec, vec],
        out_specs=[pl.BlockSpec((t, D_MODEL), lambda i, e: (i, 0))] * 2,
        out_shape=[jax.ShapeDtypeStruct((n, D_MODEL), F32), jax.ShapeDtypeStruct((n, D_MODEL), BF16)],
        scratch_shapes=[pltpu.VMEM((D_MODEL, t), BF16), pltpu.VMEM((D_MODEL, t), F32),
                        pltpu.VMEM((PEER_EC, t), F32), pltpu.VMEM((PEER_EC, t), F32),
                        pltpu.VMEM((PEER_EC, t), BF16), pltpu.VMEM((PEER_EC, t), BF16)],
        compiler_params=_cparams(("parallel", "arbitrary")),
        name="peer_dense",
    )(x, u, vt, cnt, a, rk, bn, g, b)


def _rope_tables(seq, dim):
    inv = ROPE_THETA ** (-jnp.arange(0, dim, 2, dtype=F32) / dim)
    ang = jnp.arange(seq, dtype=F32)[:, None] * inv[None, :]
    return jnp.cos(ang), jnp.sin(ang)


def _retention_tables():
    log_g = jnp.log(1.0 - 2.0 ** (-5.0 - jnp.arange(RET_HEADS, dtype=F32)))
    idx = jnp.arange(RET_CHUNK, dtype=F32)
    diff = idx[:, None] - idx[None, :]
    dmat = jnp.where(diff >= 0, jnp.exp(jnp.maximum(diff, 0.0)[None] * log_g[:, None, None]), 0.0)
    xi = jnp.exp((idx[None] + 1.0) * log_g[:, None])[..., None]
    zeta = jnp.exp((RET_CHUNK - 1.0 - idx[None]) * log_g[:, None])[..., None]
    gch = jnp.exp(RET_CHUNK * log_g)[:, None, None]
    return dmat, xi, zeta, gch


def _block_diag(w):
    g, i, o = w.shape
    eye = jnp.eye(g, dtype=w.dtype)
    return jnp.einsum('gio,gh->giho', w, eye).reshape(g * i, g * o)


def kernel(x, w_in, conv_w, conv_b, lru_wa, lru_ba, lru_wx, lru_bx, lru_lambda, w_branch, w_out,
           ln1_g, ln1_b, peer_wq, peer_k1, peer_k2, peer_u, peer_v, ln2_g, ln2_b):
    batch, seq, d = x.shape
    n = batch * seq
    depth = w_in.shape[0]
    cos_a, sin_a = _rope_tables(seq, ATT_E)
    cos_a2 = jnp.concatenate([cos_a, cos_a], axis=1)
    sin_a2 = jnp.concatenate([-sin_a, sin_a], axis=1)
    cos_r, sin_r = _rope_tables(seq, RET_DK)
    dmat, xi, zeta, gch = _retention_tables()

    xf = x.reshape(n, d)
    xb = xf.astype(BF16)
    for l in range(depth):
        w_att = w_in[l][:, :D_ATT].astype(BF16)
        w_rest = w_in[l][:, D_ATT:].astype(BF16)
        wg = jnp.concatenate([_block_diag(lru_wa[l]), _block_diag(lru_wx[l])], axis=1).astype(BF16)
        bg = jnp.concatenate([lru_ba[l], lru_bx[l]])[None, :]
        wb = w_branch[l].astype(BF16)
        wba, wbb, wbc = wb[:ATT_W], wb[ATT_W:ATT_W + d], wb[ATT_W + d:]
        wo = w_out[l].astype(BF16)
        wq = peer_wq[l].astype(BF16)
        k1 = peer_k1[l].astype(BF16)
        k2 = peer_k2[l].astype(BF16)
        u = peer_u[l].astype(BF16)
        vt = peer_v[l].reshape(-1, PEER_EC, d).transpose(0, 2, 1).astype(BF16)

        att3 = _matmul_heads(xb, w_att, batch, seq)
        rest = _matmul(xb, w_rest, BF16)
        rest3 = rest.reshape(batch, seq, D_REST)
        os_, ls_ = [], []
        for g, (_, dil) in enumerate(ATT_GROUPS):
            o, lse = _attention_group(att3, cos_a2, sin_a2, g, dil)
            os_.append(o)
            ls_.append(lse)
        yb = _lru(rest3, conv_w[l], conv_b[l][None, :], wg, bg, lru_lambda[l][None, :], BF16)
        yc = _retention(rest3, cos_r, sin_r, dmat, xi, zeta, gch, BF16)
        x1 = _merge(os_, ls_, yb.reshape(n, d), yc.reshape(n, d), rest, xf, wba, wbb, wbc, wo,
                    ln1_g[l][None, :], ln1_b[l][None, :])
        cnt, a, rk, bn = _peer_select(x1, wq, k1, k2)
        xf, xb = _peer_dense(x1, u, vt, cnt, a, rk, bn, ln2_g[l][None, :], ln2_b[l][None, :])
    return xf.reshape(batch, seq, d)
```
